```python
import jax, jax.numpy as jnp
from jax import lax
import numpy as np

D_MODEL = 4096
BATCH = 32
SEQ = 256
DEPTH = 4
DEC_BATCH = 2
DEC_SEQ = 4096
PAST_LEN = 512

GRID_W = 64
D_MIX = D_MODEL
W_A = D_MIX // 4
HD_A = 64
H_A = W_A // HD_A
CONV_A = 4
C_LRU = 8.0
W_B = D_MIX // 4
CONV_B = 31
W_C = D_MIX // 2
N_C = 64
H_C = W_C // N_C
R_W = 96
R_A = 96
R_G = 256
D_TS = 3 * W_C + R_W + R_A + R_G
D_IN = 2 * W_A + 2 * W_B + D_TS
D_FF = 4 * D_MODEL
ALPHA = (2 * DEPTH) ** 0.25
BETA = (8 * DEPTH) ** -0.25
LN_EPS = 1e-5
GN_EPS = 64e-5

kernel_name = 'hybrid_lru_conformer_rwkv7_diffusion_step'


def _layernorm(x, g, b, eps=LN_EPS):
    xf = x.astype(jnp.float32)
    mu = jnp.mean(xf, -1, keepdims=True)
    var = jnp.mean(jnp.square(xf - mu), -1, keepdims=True)
    return ((xf - mu) * lax.rsqrt(var + eps)).astype(x.dtype) * g + b


def _dwconv_seq(u, w, b, pad):
    out = lax.conv_general_dilated(u, w[:, None, :], window_strides=(1,), padding=[pad],
                                   dimension_numbers=('NWC', 'WIO', 'NWC'),
                                   feature_group_count=u.shape[-1])
    return out + b


def _dwconv_grid_cols(u, w, b):
    bsz, t, ch = u.shape
    rows = t // GRID_W
    u4 = u.reshape(bsz, rows, GRID_W, ch)
    out = lax.conv_general_dilated(u4, w[:, None, None, :], window_strides=(1, 1),
                                   padding=[(CONV_B // 2, CONV_B // 2), (0, 0)],
                                   dimension_numbers=('NHWC', 'HWIO', 'NHWC'),
                                   feature_group_count=ch)
    return out.reshape(bsz, t, ch) + b


def _linear_scan(a, b, h0, reverse):
    def comb(l, r):
        return l[0] * r[0], r[0] * l[1] + r[1]
    acc_a, acc_b = lax.associative_scan(comb, (a, b), reverse=reverse, axis=1)
    return acc_a * h0[:, None, :] + acc_b


def _rglru_mixer(zx, zg, conv_w, conv_b, ga_w, ga_b, gx_w, gx_b, lam, h0):
    bsz, t, _ = zx.shape
    u = _dwconv_seq(zx, conv_w, conv_b, (1, CONV_A - 2))
    uh = u.reshape(bsz, t, H_A, HD_A)
    r = jax.nn.sigmoid(jnp.einsum('bthi,dhij->dbthj', uh, ga_w).reshape(2, bsz, t, W_A) + ga_b[:, None, None, :])
    i = jax.nn.sigmoid(jnp.einsum('bthi,dhij->dbthj', uh, gx_w).reshape(2, bsz, t, W_A) + gx_b[:, None, None, :])
    log_a = -C_LRU * r.astype(jnp.float32) * jax.nn.softplus(-lam.astype(jnp.float32))[:, None, None, :]
    a = jnp.exp(log_a)
    b = jnp.sqrt(-jnp.expm1(2.0 * log_a)) * (i * u[None]).astype(jnp.float32)
    h_f = _linear_scan(a[0], b[0], h0[:, 0], False)
    h_b = _linear_scan(a[1], b[1], h0[:, 1], True)
    y = (h_f + h_b).astype(zx.dtype) * jax.nn.gelu(zg)
    return y, h_f[:, -1], h_b[:, 0]


def _conformer_conv(z, conv_w, conv_b, ln_g, ln_b, on_grid):
    u = z[..., :W_B] * jax.nn.sigmoid(z[..., W_B:])
    if on_grid:
        u = _dwconv_grid_cols(u, conv_w, conv_b)
    else:
        u = _dwconv_seq(u, conv_w, conv_b, (CONV_B // 2, CONV_B // 2))
    return jax.nn.silu(_layernorm(u, ln_g, ln_b))


def _token_shift(z, mu_p, mu_n):
    zp = jnp.pad(z[:, :-1], ((0, 0), (1, 0), (0, 0)))
    zn = jnp.pad(z[:, 1:], ((0, 0), (0, 1), (0, 0)))
    return z + mu_p * (zp - z) + mu_n * (zn - z)


def _heads(t):
    return t.astype(jnp.float32).reshape(t.shape[:-1] + (H_C, N_C))


def _wkv_scan(r, w, k, v, kk, a, s0, reverse):
    def step(s, inp):
        r_t, w_t, k_t, v_t, kk_t, a_t = inp
        sa = jnp.einsum('bhvk,bhk->bhv', s, kk_t)
        s = (s * w_t[:, :, None, :] - sa[..., None] * (kk_t * a_t)[:, :, None, :]
             + v_t[..., None] * k_t[:, :, None, :])
        return s, jnp.einsum('bhvk,bhk->bhv', s, r_t)
    xs = tuple(jnp.moveaxis(q, 1, 0) for q in (r, w, k, v, kk, a))
    s_last, o = lax.scan(step, s0, xs, reverse=reverse)
    return jnp.moveaxis(o, 0, 1), s_last


def _rwkv7_mixer(zc, mu_p, mu_n, w0, w_up, a0, a_up, g_up, k_k, k_a, r_k, lnx_g, lnx_b, s0):
    bsz, t, _ = zc.shape
    dt = zc.dtype
    zc = _token_shift(zc, mu_p, mu_n)
    r, k, v, xw, xa, xg = jnp.split(zc, [W_C, 2 * W_C, 3 * W_C, 3 * W_C + R_W, 3 * W_C + R_W + R_A], axis=-1)
    g = jax.nn.sigmoid(xg) @ g_up
    wlog = -jax.nn.softplus(-(w0[:, None, None, :] + jnp.einsum('btr,drc->dbtc', jnp.tanh(xw), w_up))) - 0.5
    decay = jnp.exp(-jnp.exp(wlog.astype(jnp.float32)))
    a = jax.nn.sigmoid(a0[:, None, None, :] + jnp.einsum('btr,drc->dbtc', xa, a_up))
    kd = k[None] * (1 + (a - 1) * k_a)
    kk = _heads(k * k_k)
    kk = kk / jnp.maximum(jnp.sqrt(jnp.sum(jnp.square(kk), -1, keepdims=True)), 1e-12)
    rh, vh = _heads(r), _heads(v)
    ah, kdh, wh = _heads(a), _heads(kd), _heads(decay)
    o_f, s_f = _wkv_scan(rh, wh[0], kdh[0], vh, kk, ah[0], s0[:, 0], False)
    o_b, s_b = _wkv_scan(rh, wh[1], kdh[1], vh, kk, ah[1], s0[:, 1], True)
    o = o_f + o_b
    mu = jnp.mean(o, -1, keepdims=True)
    var = jnp.mean(jnp.square(o - mu), -1, keepdims=True)
    on = ((o - mu) * lax.rsqrt(var + GN_EPS)).reshape(bsz, t, W_C).astype(dt) * lnx_g + lnx_b
    bonus = (jnp.sum(rh * (kdh[0] + kdh[1]) * r_k, -1, keepdims=True) * vh).reshape(bsz, t, W_C)
    y = (on + bonus.astype(dt)) * g
    return y, s_f, s_b


def _layer(x, mod, p, h0_lru, s0_wkv, on_grid):
    sh1, sc1, g1, sh2, sc2, g2 = jnp.split(mod, 6, axis=-1)
    h = x * (1 + sc1) + sh1
    z = h @ p['w_in']
    zx, zg, zb, zc = jnp.split(z, [W_A, 2 * W_A, 2 * W_A + 2 * W_B], axis=-1)
    ya, hf, hb = _rglru_mixer(zx, zg, p['conv_a_w'], p['conv_a_b'], p['lru_ga_w'], p['lru_ga_b'],
                              p['lru_gx_w'], p['lru_gx_b'], p['lru_lambda'], h0_lru)
    yb = _conformer_conv(zb, p['conv_b_w'], p['conv_b_b'], p['conv_ln_g'], p['conv_ln_b'], on_grid)
    yc, sf, sb = _rwkv7_mixer(zc, p['ts_mu_prev'], p['ts_mu_next'], p['rwkv_w0'], p['rwkv_w_up'],
                              p['rwkv_a0'], p['rwkv_a_up'], p['rwkv_g_up'], p['rwkv_k_k'], p['rwkv_k_a'],
                              p['rwkv_r_k'], p['rwkv_lnx_g'], p['rwkv_lnx_b'], s0_wkv)
    y = jnp.concatenate([ya, yb, yc], axis=-1) @ p['w_out']
    x = _layernorm(ALPHA * x + g1 * y, p['ln1_g'], p['ln1_b'])
    h = x * (1 + sc2) + sh2
    f = jnp.square(jax.nn.relu(h @ p['w_up'])) @ p['w_down']
    x = _layernorm(ALPHA * x + g2 * f, p['ln2_g'], p['ln2_b'])
    return x, jnp.stack([hf, hb], axis=1), jnp.stack([sf, sb], axis=1)


def setup_inputs(seed: int = 0) -> dict:
    key = jax.random.key(seed)
    ks = iter(jax.random.split(key, 64))
    f32 = jnp.float32
    L = DEPTH

    def nrm(shape, s):
        return s * jax.random.normal(next(ks), shape, f32)

    def unif(shape, lo, hi):
        return jax.random.uniform(next(ks), shape, f32, lo, hi)

    a_lru = unif((L, 2, W_A), 0.9, 0.999) ** (1.0 / C_LRU)
    return {
        'x_prompt': nrm((BATCH, SEQ, D_MODEL), 1.0),
        'x_sample': nrm((DEC_BATCH, DEC_SEQ, D_MODEL), 1.0),
        'state_lru': nrm((DEC_BATCH, L, 2, W_A), 0.5),
        'state_wkv': nrm((DEC_BATCH, L, 2, H_C, N_C, N_C), 0.3),
        'c': nrm((DEC_BATCH, D_MODEL), 1.0),
        'c_ctx': nrm((D_MODEL,), 1.0),
        'ln_in_g': 1.0 + nrm((D_MODEL,), 0.02),
        'ln_in_b': nrm((D_MODEL,), 0.02),
        'w_mod': nrm((L, D_MODEL, 6 * D_MODEL), 0.5 * D_MODEL ** -0.5),
        'b_mod': nrm((L, 6 * D_MODEL), 0.02),
        'w_in': nrm((L, D_MODEL, D_IN), D_MODEL ** -0.5),
        'w_out': nrm((L, D_MIX, D_MODEL), BETA * D_MIX ** -0.5),
        'conv_a_w': nrm((L, CONV_A, W_A), CONV_A ** -0.5),
        'conv_a_b': nrm((L, W_A), 0.02),
        'lru_ga_w': nrm((L, 2, H_A, HD_A, HD_A), HD_A ** -0.5),
        'lru_ga_b': nrm((L, 2, W_A), 0.02),
        'lru_gx_w': nrm((L, 2, H_A, HD_A, HD_A), HD_A ** -0.5),
        'lru_gx_b': nrm((L, 2, W_A), 0.02),
        'lru_lambda': jnp.log(a_lru) - jnp.log1p(-a_lru),
        'conv_b_w': nrm((L, CONV_B, W_B), CONV_B ** -0.5),
        'conv_b_b': nrm((L, W_B), 0.02),
        'conv_ln_g': 1.0 + nrm((L, W_B), 0.02),
        'conv_ln_b': nrm((L, W_B), 0.02),
        'ts_mu_prev': unif((L, D_TS), 0.0, 0.5),
        'ts_mu_next': unif((L, D_TS), 0.0, 0.5),
        'rwkv_w0': unif((L, 2, W_C), -6.0, 1.0),
        'rwkv_w_up': nrm((L, 2, R_W, W_C), 0.1 * R_W ** -0.5),
        'rwkv_a0': nrm((L, 2, W_C), 0.1),
        'rwkv_a_up': nrm((L, 2, R_A, W_C), 0.1 * R_A ** -0.5),
        'rwkv_g_up': nrm((L, R_G, W_C), R_G ** -0.5),
        'rwkv_k_k': 0.85 + nrm((L, W_C), 0.05),
        'rwkv_k_a': 1.0 + nrm((L, W_C), 0.05),
        'rwkv_r_k': nrm((L, H_C, N_C), 0.1),
        'rwkv_lnx_g': 1.0 + nrm((L, W_C), 0.02),
        'rwkv_lnx_b': nrm((L, W_C), 0.02),
        'ln1_g': 1.0 + nrm((L, D_MODEL), 0.02),
        'ln1_b': nrm((L, D_MODEL), 0.02),
        'w_up': nrm((L, D_MODEL, D_FF), D_MODEL ** -0.5),
        'w_down': nrm((L, D_FF, D_MODEL), BETA * D_FF ** -0.5),
        'ln2_g': 1.0 + nrm((L, D_MODEL), 0.02),
        'ln2_b': nrm((L, D_MODEL), 0.02),
    }


def reference(x_prompt, x_sample, state_lru, state_wkv, c, c_ctx, ln_in_g, ln_in_b, w_mod, b_mod,
              w_in, w_out, conv_a_w, conv_a_b, lru_ga_w, lru_ga_b, lru_gx_w, lru_gx_b, lru_lambda,
              conv_b_w, conv_b_b, conv_ln_g, conv_ln_b, ts_mu_prev, ts_mu_next, rwkv_w0, rwkv_w_up,
              rwkv_a0, rwkv_a_up, rwkv_g_up, rwkv_k_k, rwkv_k_a, rwkv_r_k, rwkv_lnx_g, rwkv_lnx_b,
              ln1_g, ln1_b, w_up, w_down, ln2_g, ln2_b):
    xp = _layernorm(x_prompt, ln_in_g, ln_in_b)
    xs = _layernorm(x_sample, ln_in_g, ln_in_b)
    bp = xp.shape[0]
    h0_ctx = jnp.zeros((bp, 2, W_A), jnp.float32)
    s0_ctx = jnp.zeros((bp, 2, H_C, N_C, N_C), jnp.float32)
    lru_states, wkv_states = [], []
    for l in range(DEPTH):
        p = {
            'w_in': w_in[l], 'w_out': w_out[l],
            'conv_a_w': conv_a_w[l], 'conv_a_b': conv_a_b[l],
            'lru_ga_w': lru_ga_w[l], 'lru_ga_b': lru_ga_b[l],
            'lru_gx_w': lru_gx_w[l], 'lru_gx_b': lru_gx_b[l], 'lru_lambda': lru_lambda[l],
            'conv_b_w': conv_b_w[l], 'conv_b_b': conv_b_b[l],
            'conv_ln_g': conv_ln_g[l], 'conv_ln_b': conv_ln_b[l],
            'ts_mu_prev': ts_mu_prev[l], 'ts_mu_next': ts_mu_next[l],
            'rwkv_w0': rwkv_w0[l], 'rwkv_w_up': rwkv_w_up[l],
            'rwkv_a0': rwkv_a0[l], 'rwkv_a_up': rwkv_a_up[l], 'rwkv_g_up': rwkv_g_up[l],
            'rwkv_k_k': rwkv_k_k[l], 'rwkv_k_a': rwkv_k_a[l], 'rwkv_r_k': rwkv_r_k[l],
            'rwkv_lnx_g': rwkv_lnx_g[l], 'rwkv_lnx_b': rwkv_lnx_b[l],
            'ln1_g': ln1_g[l], 'ln1_b': ln1_b[l],
            'w_up': w_up[l], 'w_down': w_down[l],
            'ln2_g': ln2_g[l], 'ln2_b': ln2_b[l],
        }
        mod_ctx = (jax.nn.silu(c_ctx) @ w_mod[l] + b_mod[l])[None, None, :]
        mod_lat = (jax.nn.silu(c) @ w_mod[l] + b_mod[l])[:, None, :]
        xp, st_lru, st_wkv = _layer(xp, mod_ctx, p, h0_ctx, s0_ctx, False)
        lru_states.append(st_lru)
        wkv_states.append(st_wkv)
        xs = _layer(xs, mod_lat, p, state_lru[:, l].astype(jnp.float32),
                    state_wkv[:, l].astype(jnp.float32), True)[0]
    new_state_lru = jnp.stack(lru_states, axis=1)
    new_state_wkv = jnp.stack(wkv_states, axis=1)
    return (xp, xs, new_state_lru, new_state_wkv)
```

```python
import functools
import math

import jax
import jax.numpy as jnp
from jax import lax
from jax.experimental import pallas as pl
from jax.experimental.pallas import tpu as pltpu

F32 = jnp.float32
BF16 = jnp.bfloat16
HI = lax.Precision.HIGHEST

LANES = 128
SUBLANES = 8
VMEM_LIMIT = 60 * 1024 * 1024
GRID_W = 64
HEAD = 64
CHUNK = 64
C_LRU = 8.0
LN_EPS = 1e-5
GN_EPS = 64e-5


def _pick(n, cands):
    for c in cands:
        if c <= n and n % c == 0:
            return c
    return n


def _round_up(n, m):
    return (n + m - 1) // m * m


def _params(*sem):
    return pltpu.CompilerParams(dimension_semantics=sem, vmem_limit_bytes=VMEM_LIMIT)


def _sigmoid(x):
    return 1.0 / (1.0 + jnp.exp(-x))


def _softplus(x):
    return jnp.maximum(x, 0.0) + jnp.log(1.0 + jnp.exp(-jnp.abs(x)))


def _gelu_tanh(x):
    return 0.5 * x * (1.0 + jnp.tanh(math.sqrt(2.0 / math.pi) * (x + 0.044715 * (x * x * x))))


def _head_ones():
    r = lax.broadcasted_iota(jnp.int32, (LANES, LANES), 0) // HEAD
    c = lax.broadcasted_iota(jnp.int32, (LANES, LANES), 1) // HEAD
    return (r == c).astype(F32)


def _head_sum(x, ones):
    parts = [jnp.dot(x[:, j * LANES:(j + 1) * LANES], ones, precision=HI, preferred_element_type=F32)
             for j in range(x.shape[1] // LANES)]
    return parts[0] if len(parts) == 1 else jnp.concatenate(parts, axis=1)


def _mod_body(c_ref, w_ref, b_ref, o_ref):
    c = c_ref[...]
    a = (c * _sigmoid(c)).astype(BF16)
    o_ref[0] = jnp.dot(a, w_ref[0].astype(BF16), preferred_element_type=F32) + b_ref[0]


def _modulation(cc, w_mod, b_mod):
    depth, d, n = w_mod.shape
    rows = cc.shape[0]
    tn = _pick(n, (512, 256, 128))
    return pl.pallas_call(
        _mod_body,
        grid=(depth, n // tn),
        in_specs=[pl.BlockSpec((rows, d), lambda l, j: (0, 0)),
                  pl.BlockSpec((1, d, tn), lambda l, j: (l, 0, j)),
                  pl.BlockSpec((1, 1, tn), lambda l, j: (l, 0, j))],
        out_specs=pl.BlockSpec((1, rows, tn), lambda l, j: (l, 0, j)),
        out_shape=jax.ShapeDtypeStruct((depth, rows, n), F32),
        compiler_params=_params("parallel", "parallel"),
        name="modulation",
    )(cc, w_mod, b_mod.reshape(depth, 1, n))


def _mm_body(a_ref, w_ref, o_ref, *scratch, nk, relu2):
    def finish(r):
        if relu2:
            r = jnp.square(jnp.maximum(r, 0.0))
        o_ref[...] = r.astype(o_ref.dtype)

    if nk == 1:
        finish(jnp.dot(a_ref[...], w_ref[...], preferred_element_type=F32))
        return
    acc_ref, = scratch
    k = pl.program_id(2)

    @pl.when(k == 0)
    def _():
        acc_ref[...] = jnp.zeros_like(acc_ref)

    acc_ref[...] += jnp.dot(a_ref[...], w_ref[...], preferred_element_type=F32)

    @pl.when(k == nk - 1)
    def _():
        finish(acc_ref[...])


def _matmul(a, w, out_dtype, relu2=False, name="matmul"):
    m, kdim = a.shape
    n = w.shape[1]
    tm = _pick(m, (1024, 512, 256, 128, 64, 32, 16))
    tn = _pick(n, (512, 256, 128))
    tk = _pick(kdim, (4096, 2048, 1024, 512, 256, 128))
    nk = kdim // tk
    return pl.pallas_call(
        functools.partial(_mm_body, nk=nk, relu2=relu2),
        grid=(m // tm, n // tn, nk),
        in_specs=[pl.BlockSpec((tm, tk), lambda i, j, k: (i, k)),
                  pl.BlockSpec((tk, tn), lambda i, j, k: (k, j))],
        out_specs=pl.BlockSpec((tm, tn), lambda i, j, k: (i, j)),
        out_shape=jax.ShapeDtypeStruct((m, n), out_dtype),
        scratch_shapes=[pltpu.VMEM((tm, tn), F32)] if nk > 1 else [],
        compiler_params=_params("parallel", "parallel", "arbitrary"),
        name=name,
    )(a, w)


def _ln_body(*refs, has_res, has_mod, alpha):
    it = iter(refs)
    x_ref = next(it)
    if has_res:
        y_ref, gate_ref = next(it), next(it)
    g_ref, b_ref = next(it), next(it)
    if has_mod:
        sc_ref, sh_ref = next(it), next(it)
    xo_ref = next(it)
    x = x_ref[...]
    if has_res:
        x = alpha * x + gate_ref[0] * y_ref[...]
    mu = jnp.mean(x, axis=-1, keepdims=True)
    xc = x - mu
    var = jnp.mean(xc * xc, axis=-1, keepdims=True)
    xn = xc * lax.rsqrt(var + LN_EPS) * g_ref[...] + b_ref[...]
    xo_ref[...] = xn
    if has_mod:
        h_ref = next(it)
        h_ref[...] = (xn * (1.0 + sc_ref[0]) + sh_ref[0]).astype(BF16)


def _ln_mod(x, seq_len, g, b, res=None, mod=None, alpha=1.0):
    m, d = x.shape
    tm = _pick(seq_len, (256, 128, 64, 32, 16))

    def per_seq(arr):
        if arr.shape[0] == 1:
            return pl.BlockSpec((1, 1, d), lambda i: (0, 0, 0))
        return pl.BlockSpec((1, 1, d), lambda i: ((i * tm) // seq_len, 0, 0))

    row = pl.BlockSpec((tm, d), lambda i: (i, 0))
    vec = pl.BlockSpec((1, d), lambda i: (0, 0))
    args, specs = [x], [row]
    if res is not None:
        args += [res[0], res[1]]
        specs += [row, per_seq(res[1])]
    args += [g.reshape(1, d), b.reshape(1, d)]
    specs += [vec, vec]
    if mod is not None:
        args += [mod[0], mod[1]]
        specs += [per_seq(mod[0]), per_seq(mod[1])]
    out_shape = [jax.ShapeDtypeStruct((m, d), F32)]
    out_specs = [row]
    if mod is not None:
        out_shape.append(jax.ShapeDtypeStruct((m, d), BF16))
        out_specs.append(row)
    out = pl.pallas_call(
        functools.partial(_ln_body, has_res=res is not None, has_mod=mod is not None, alpha=alpha),
        grid=(m // tm,),
        in_specs=specs,
        out_specs=out_specs,
        out_shape=out_shape,
        compiler_params=_params("parallel"),
        name="ln_mod",
    )(*args)
    return out if mod is not None else (out[0], None)


def _halo_specs(tm, width, col, m):
    nb8 = m // SUBLANES
    r8 = tm // SUBLANES
    return [pl.BlockSpec((tm, width), lambda i: (i, col)),
            pl.BlockSpec((SUBLANES, width), lambda i: (jnp.maximum(i * r8 - 1, 0), col)),
            pl.BlockSpec((SUBLANES, width), lambda i: (jnp.minimum((i + 1) * r8, nb8 - 1), col))]


def _fill_ext(ext_ref, z_ref, zp_ref, zn_ref, tm, seq_len):
    i = pl.program_id(0)
    first = (i * tm) % seq_len == 0
    last = ((i + 1) * tm) % seq_len == 0
    ext_ref[0:SUBLANES] = jnp.where(first, 0.0, zp_ref[...])
    ext_ref[SUBLANES:SUBLANES + tm] = z_ref[...]
    ext_ref[SUBLANES + tm:2 * SUBLANES + tm] = jnp.where(last, 0.0, zn_ref[...])


def _lru_gates_body(zx_ref, zp_ref, zn_ref, cw_ref, cb_ref, gaw_ref, gab_ref, gxw_ref, gxb_ref, lam_ref,
                    o_ref, ext_ref, *, tm, seq_len, nj, taps):
    _fill_ext(ext_ref, zx_ref, zp_ref, zn_ref, tm, seq_len)
    u = cb_ref[...] + cw_ref[0:1, :] * ext_ref[SUBLANES - 1:SUBLANES - 1 + tm, :]
    for k in range(1, taps):
        u = u + cw_ref[k:k + 1, :] * ext_ref[SUBLANES - 1 + k:SUBLANES - 1 + k + tm, :]
    for j in range(nj):
        sl = slice(j * LANES, (j + 1) * LANES)
        uj = u[:, sl]
        ub = uj.astype(BF16)
        for d in range(2):
            r = _sigmoid(jnp.dot(ub, gaw_ref[d, j], preferred_element_type=F32) + gab_ref[d][:, sl])
            ig = _sigmoid(jnp.dot(ub, gxw_ref[d, j], preferred_element_type=F32) + gxb_ref[d][:, sl])
            log_a = (-C_LRU) * r * _softplus(-lam_ref[d][:, sl])
            a = jnp.exp(log_a)
            bb = jnp.sqrt(-jnp.tanh(log_a) * (a * a + 1.0)) * (ig * uj)
            o_ref[2 * d, :, sl] = a
            o_ref[2 * d + 1, :, sl] = bb


def _lru_gates(z_ab, seq_len, p):
    m = z_ab.shape[0]
    wa = p["conv_a_w"].shape[1]
    taps = p["conv_a_w"].shape[0]
    nj = wa // LANES
    tm = _pick(seq_len, (256, 128, 64, 32, 16))
    full = lambda a: pl.BlockSpec(a.shape, lambda i: (0,) * a.ndim)
    consts = [p["conv_a_w"], p["conv_a_b"], p["lru_ga_w"], p["lru_ga_b"], p["lru_gx_w"], p["lru_gx_b"],
              p["lru_lambda"]]
    return pl.pallas_call(
        functools.partial(_lru_gates_body, tm=tm, seq_len=seq_len, nj=nj, taps=taps),
        grid=(m // tm,),
        in_specs=_halo_specs(tm, wa, 0, m) + [full(a) for a in consts],
        out_specs=pl.BlockSpec((4, tm, wa), lambda i: (0, i, 0)),
        out_shape=jax.ShapeDtypeStruct((4, m, wa), F32),
        scratch_shapes=[pltpu.VMEM((tm + 2 * SUBLANES, wa), F32)],
        compiler_params=_params("parallel"),
        name="lru_gates",
    )(z_ab, z_ab, z_ab, *consts)


def _lru_scan_body(af_ref, bf_ref, ab_ref, bb_ref, h0_ref, hf_ref, hb_ref, st_ref, carry_ref, *, tt, nt):
    i = pl.program_id(1)

    @pl.when(i == 0)
    def _():
        carry_ref[...] = h0_ref[0]

    def step(t, c):
        hf, hb = c
        hf = af_ref[0, 0, t] * hf + bf_ref[0, 0, t]
        hf_ref[0, t] = hf
        tb = tt - 1 - t
        hb = ab_ref[0, 0, tb] * hb + bb_ref[0, 0, tb]
        hb_ref[0, tb] = hb
        return hf, hb

    hf, hb = lax.fori_loop(0, tt, step, (carry_ref[0], carry_ref[1]), unroll=8)
    carry_ref[0] = hf
    carry_ref[1] = hb

    @pl.when(i == nt - 1)
    def _():
        st_ref[0, 0] = hf
        st_ref[0, 1] = hb


def _lru_scan(ab, h0, bsz, seq_len):
    wa = ab.shape[-1]
    s8 = wa // LANES
    ab5 = ab.reshape(4, bsz, seq_len, s8, LANES)
    h0 = h0.reshape(bsz, 2, s8, LANES)
    tt = _pick(seq_len, (256, 128, 64, 32, 16, 8))
    nt = seq_len // tt
    blk = (1, 1, tt, s8, LANES)
    oblk = (1, tt, s8, LANES)
    hf, hb, st = pl.pallas_call(
        functools.partial(_lru_scan_body, tt=tt, nt=nt),
        grid=(bsz, nt),
        in_specs=[pl.BlockSpec(blk, lambda b, i: (0, b, i, 0, 0)),
                  pl.BlockSpec(blk, lambda b, i: (1, b, i, 0, 0)),
                  pl.BlockSpec(blk, lambda b, i: (2, b, nt - 1 - i, 0, 0)),
                  pl.BlockSpec(blk, lambda b, i: (3, b, nt - 1 - i, 0, 0)),
                  pl.BlockSpec((1, 2, s8, LANES), lambda b, i: (b, 0, 0, 0))],
        out_specs=[pl.BlockSpec(oblk, lambda b, i: (b, i, 0, 0)),
                   pl.BlockSpec(oblk, lambda b, i: (b, nt - 1 - i, 0, 0)),
                   pl.BlockSpec((1, 2, s8, LANES), lambda b, i: (b, 0, 0, 0))],
        out_shape=[jax.ShapeDtypeStruct((bsz, seq_len, s8, LANES), F32),
                   jax.ShapeDtypeStruct((bsz, seq_len, s8, LANES), F32),
                   jax.ShapeDtypeStruct((bsz, 2, s8, LANES), F32)],
        scratch_shapes=[pltpu.VMEM((2, s8, LANES), F32)],
        compiler_params=_params("parallel", "arbitrary"),
        name="lru_scan",
    )(ab5, ab5, ab5, ab5, h0)
    m = bsz * seq_len
    return hf.reshape(m, wa), hb.reshape(m, wa), st.reshape(bsz, 2, wa)


def _convb_body(zv_ref, zg_ref, w_ref, b_ref, o_ref, pad_ref, *, seq_len, pad, stride, taps, rc):
    zeros = jnp.zeros((pad, LANES), F32)
    pad_ref[0:pad] = zeros
    pad_ref[pad + seq_len:2 * pad + seq_len] = zeros
    pad_ref[pad:pad + seq_len] = zv_ref[0] * _sigmoid(zg_ref[0])
    half = taps // 2

    def rows(r0):
        acc = jnp.broadcast_to(b_ref[...], (rc, LANES))
        for k in range(taps):
            acc = acc + w_ref[k:k + 1, :] * pad_ref[pl.ds(r0 + pad + (k - half) * stride, rc), :]
        o_ref[0, pl.ds(r0, rc), :] = acc

    nchunks = seq_len // rc
    if stride == 1:
        for c in range(nchunks):
            rows(c * rc)
    else:
        def chunk(c, carry):
            rows(pl.multiple_of(c * rc, rc))
            return carry
        lax.fori_loop(0, nchunks, chunk, 0)


def _convb(z_ab, bsz, seq_len, col0, p, on_grid):
    wb = p["conv_b_w"].shape[1]
    taps = p["conv_b_w"].shape[0]
    nj = wb // LANES
    stride = GRID_W if on_grid else 1
    pad = _round_up((taps // 2) * stride, SUBLANES * 2)
    rc = _pick(seq_len, (64, 32, 16, 8)) if stride == 1 else _pick(seq_len, (256, 128, 64))
    z3 = z_ab.reshape(bsz, seq_len, z_ab.shape[1])
    cv, cg = col0 // LANES, (col0 + wb) // LANES
    out = pl.pallas_call(
        functools.partial(_convb_body, seq_len=seq_len, pad=pad, stride=stride, taps=taps, rc=rc),
        grid=(bsz, nj),
        in_specs=[pl.BlockSpec((1, seq_len, LANES), lambda b, j: (b, 0, cv + j)),
                  pl.BlockSpec((1, seq_len, LANES), lambda b, j: (b, 0, cg + j)),
                  pl.BlockSpec((taps, LANES), lambda b, j: (0, j)),
                  pl.BlockSpec((1, LANES), lambda b, j: (0, j))],
        out_specs=pl.BlockSpec((1, seq_len, LANES), lambda b, j: (b, 0, j)),
        out_shape=jax.ShapeDtypeStruct((bsz, seq_len, wb), F32),
        scratch_shapes=[pltpu.VMEM((seq_len + 2 * pad, LANES), F32)],
        compiler_params=_params("parallel", "parallel"),
        name="convb_grid" if on_grid else "convb_seq",
    )(z3, z3, p["conv_b_w"], p["conv_b_b"])
    return out.reshape(bsz * seq_len, wb)


def _prep_body(z_ref, zp_ref, zn_ref, mup_ref, mun_ref, w0_ref, wup_ref, a0_ref, aup_ref, gup_ref,
               kk_ref, ka_ref, rk_ref, sh_ref, pd_ref, ext_ref, *, tm, seq_len, wc, rw, ra):
    _fill_ext(ext_ref, z_ref, zp_ref, zn_ref, tm, seq_len)
    ones = _head_ones()

    def shifted(c0, c1):
        z = ext_ref[SUBLANES:SUBLANES + tm, c0:c1]
        zp = ext_ref[SUBLANES - 1:SUBLANES - 1 + tm, c0:c1]
        zn = ext_ref[SUBLANES + 1:SUBLANES + 1 + tm, c0:c1]
        return z + mup_ref[:, c0:c1] * (zp - z) + mun_ref[:, c0:c1] * (zn - z)

    r = shifted(0, wc)
    k = shifted(wc, 2 * wc)
    v = shifted(2 * wc, 3 * wc)
    xw = shifted(3 * wc, 3 * wc + rw)
    xa = shifted(3 * wc + rw, 3 * wc + rw + ra)
    xg = shifted(3 * wc + rw + ra, z_ref.shape[1])

    sh_ref[0] = r
    sh_ref[1] = v
    kk = k * kk_ref[...]
    kap = kk / jnp.maximum(jnp.sqrt(_head_sum(kk * kk, ones)), 1e-12)
    sh_ref[2] = kap
    sh_ref[3] = jnp.dot(_sigmoid(xg).astype(BF16), gup_ref[...], preferred_element_type=F32)

    tw = jnp.tanh(xw).astype(BF16)
    xab = xa.astype(BF16)
    kdsum = None
    for d in range(2):
        x = w0_ref[d] + jnp.dot(tw, wup_ref[d], preferred_element_type=F32)
        pd_ref[d, 0] = (-math.exp(-0.5)) * _sigmoid(x)
        eta = _sigmoid(a0_ref[d] + jnp.dot(xab, aup_ref[d], preferred_element_type=F32))
        pd_ref[d, 1] = kap * eta
        kd = k * (1.0 + (eta - 1.0) * ka_ref[...])
        pd_ref[d, 2] = kd
        kdsum = kd if kdsum is None else kdsum + kd
    sh_ref[4] = _head_sum(r * kdsum * rk_ref[...], ones) * v


def _rwkv_prep(z_c, seq_len, p):
    m, wz = z_c.shape
    wc = p["rwkv_k_k"].shape[1]
    rw, ra = p["rwkv_w_up"].shape[1], p["rwkv_a_up"].shape[1]
    tm = _pick(seq_len, (128, 64, 32, 16))
    full = lambda a: pl.BlockSpec(a.shape, lambda i: (0,) * a.ndim)
    consts = [p["ts_mu_prev"], p["ts_mu_next"], p["rwkv_w0"], p["rwkv_w_up"], p["rwkv_a0"], p["rwkv_a_up"],
              p["rwkv_g_up"], p["rwkv_k_k"], p["rwkv_k_a"], p["rwkv_r_k"]]
    return pl.pallas_call(
        functools.partial(_prep_body, tm=tm, seq_len=seq_len, wc=wc, rw=rw, ra=ra),
        grid=(m // tm,),
        in_specs=_halo_specs(tm, wz, 0, m) + [full(a) for a in consts],
        out_specs=[pl.BlockSpec((5, tm, wc), lambda i: (0, i, 0)),
                   pl.BlockSpec((2, 3, tm, wc), lambda i: (0, 0, i, 0))],
        out_shape=[jax.ShapeDtypeStruct((5, m, wc), F32),
                   jax.ShapeDtypeStruct((2, 3, m, wc), F32)],
        scratch_shapes=[pltpu.VMEM((tm + 2 * SUBLANES, wz), F32)],
        compiler_params=_params("parallel"),
        name="rwkv_prep",
    )(z_c, z_c, z_c, *consts)


_NT = (((1,), (1,)), ((), ()))
_TN = (((0,), (0,)), ((), ()))


def _dot(x, y):
    return jnp.dot(x, y, precision=HI, preferred_element_type=F32)


def _dotg(x, y, dims):
    return lax.dot_general(x, y, dims, precision=HI, preferred_element_type=F32)


def _wkv_body(r_ref, v_ref, kap_ref, lw_ref, ke_ref, kd_ref, s0_ref, o_ref, so_ref,
              s_ref, a_sc, b_sc, k_sc, r_sc, b2_sc, k2_sc, g_sc, *, nc, npairs, unroll):
    c = CHUNK
    d = pl.program_id(1)
    ci = pl.program_id(2)

    @pl.when(ci == 0)
    def _():
        zero = jnp.zeros((HEAD, HEAD), F32)
        for p in range(npairs):
            top = jnp.concatenate([s0_ref[0, 0, 2 * p], zero], axis=1)
            bot = jnp.concatenate([zero, s0_ref[0, 0, 2 * p + 1]], axis=1)
            s_ref[p] = jnp.concatenate([top, bot], axis=0)

    row = lax.broadcasted_iota(jnp.int32, (c, 2 * c), 0)
    col = lax.broadcasted_iota(jnp.int32, (c, 2 * c), 1) % c
    fwd = d == 0
    ahead = jnp.where(fwd, row - col, col - row)
    strict2 = ahead > 0
    incl2 = ahead >= 0
    lw = lw_ref[0, 0]
    cum = _dot(incl2[:, :c].astype(F32), lw)
    tot = jnp.where(fwd, cum[c - 1:c, :], cum[0:1, :])
    kap = kap_ref[0]
    ke = ke_ref[0, 0]
    kd = kd_ref[0, 0]
    a_sc[...] = -kap * jnp.exp(cum - lw)
    em = jnp.exp(-cum)
    b_sc[...] = ke * em
    k_sc[...] = kd * em
    r_sc[...] = r_ref[0] * jnp.exp(cum)
    e2 = jnp.exp(tot - cum)
    b2_sc[...] = ke * e2
    k2_sc[...] = kd * e2
    g_sc[...] = jnp.broadcast_to(jnp.exp(tot), g_sc.shape)

    lane = lax.broadcasted_iota(jnp.int32, (1, LANES), 1)
    m0 = lane < HEAD
    rbd = lax.broadcasted_iota(jnp.int32, (LANES, LANES), 0) // HEAD
    cbd = lax.broadcasted_iota(jnp.int32, (LANES, LANES), 1) // HEAD
    bdmask = rbd == cbd

    def stack(x):
        return jnp.concatenate([jnp.where(m0, x, 0.0), jnp.where(m0, 0.0, x)], axis=0)

    def pair(p, carry):
        sl = pl.ds(pl.multiple_of(p * LANES, LANES), LANES)
        ap, bp, kp, rp = a_sc[:, sl], b_sc[:, sl], k_sc[:, sl], r_sc[:, sl]
        vp = v_ref[0, :, sl]
        sbd = s_ref[p]
        vst = stack(vp)
        lmat = _dotg(jnp.concatenate([ap, rp], axis=0),
                     jnp.concatenate([stack(bp), stack(kp)], axis=0), _NT)
        mab = jnp.where(strict2, lmat[:c, :2 * c], 0.0)
        lak = jnp.where(strict2, lmat[:c, 2 * c:], 0.0)
        lrb = jnp.where(incl2, lmat[c:, :2 * c], 0.0)
        lrk = jnp.where(incl2, lmat[c:, 2 * c:], 0.0)
        x = _dotg(ap, sbd, _NT) + _dot(lak, vst)
        mm = mab
        levels = c.bit_length() - 1
        for lvl in range(levels):
            x = x + _dot(mm, stack(x))
            if lvl + 1 < levels:
                mm = _dot(mm, stack(mm))
        o_ref[0, :, sl] = _dotg(rp, sbd, _NT) + _dot(lrb, stack(x)) + _dot(lrk, vst)
        upd = _dotg(x, b2_sc[:, sl], _TN) + _dotg(vp, k2_sc[:, sl], _TN)
        s_ref[p] = sbd * g_sc[0:1, sl] + jnp.where(bdmask, upd, 0.0)
        return carry

    lax.fori_loop(0, npairs, pair, 0, unroll=unroll)

    @pl.when(ci == nc - 1)
    def _():
        for p in range(npairs):
            sbd = s_ref[p]
            so_ref[0, 0, 2 * p] = sbd[:HEAD, :HEAD]
            so_ref[0, 0, 2 * p + 1] = sbd[HEAD:, HEAD:]


def _wkv(shared, perdir, s0, bsz, seq_len):
    m, wc = shared.shape[1], shared.shape[2]
    nheads = wc // HEAD
    npairs = nheads // 2
    c = CHUNK
    nc = seq_len // c

    def chunk(b, d, i):
        return b * nc + jnp.where(d == 0, i, nc - 1 - i)

    sh = lambda j: pl.BlockSpec((1, c, wc), lambda b, d, i: (j, chunk(b, d, i), 0))
    pd = lambda j: pl.BlockSpec((1, 1, c, wc), lambda b, d, i: (d, j, chunk(b, d, i), 0))
    st = pl.BlockSpec((1, 1, nheads, HEAD, HEAD), lambda b, d, i: (b, d, 0, 0, 0))
    cw = pltpu.VMEM((c, wc), F32)
    return pl.pallas_call(
        functools.partial(_wkv_body, nc=nc, npairs=npairs, unroll=2 if npairs % 2 == 0 else 1),
        grid=(bsz, 2, nc),
        in_specs=[sh(0), sh(1), sh(2), pd(0), pd(1), pd(2), st],
        out_specs=[pl.BlockSpec((1, c, wc), lambda b, d, i: (d, chunk(b, d, i), 0)), st],
        out_shape=[jax.ShapeDtypeStruct((2, m, wc), F32),
                   jax.ShapeDtypeStruct((bsz, 2, nheads, HEAD, HEAD), F32)],
        scratch_shapes=[pltpu.VMEM((npairs, LANES, LANES), F32), cw, cw, cw, cw, cw, cw,
                        pltpu.VMEM((SUBLANES, wc), F32)],
        compiler_params=_params("parallel", "parallel", "arbitrary"),
        name="wkv_scan",
    )(shared, shared, shared, perdir, perdir, perdir, s0)


def _post_body(o_ref, g_ref, bonus_ref, hf_ref, hb_ref, zg_ref, cb_ref, lng_ref, lnb_ref, xg_ref, xb_ref,
               y_ref, *, wa, wb):
    ones = _head_ones()
    y_ref[:, 0:wa] = ((hf_ref[...] + hb_ref[...]) * _gelu_tanh(zg_ref[...])).astype(BF16)

    u = cb_ref[...]
    mu = jnp.mean(u, axis=-1, keepdims=True)
    uc = u - mu
    var = jnp.mean(uc * uc, axis=-1, keepdims=True)
    un = uc * lax.rsqrt(var + LN_EPS) * lng_ref[...] + lnb_ref[...]
    y_ref[:, wa:wa + wb] = (un * _sigmoid(un)).astype(BF16)

    o = o_ref[0] + o_ref[1]
    inv = 1.0 / HEAD
    omu = _head_sum(o, ones) * inv
    oc = o - omu
    ovar = _head_sum(oc * oc, ones) * inv
    on = oc * lax.rsqrt(ovar + GN_EPS) * xg_ref[...] + xb_ref[...]
    y_ref[:, wa + wb:] = ((on + bonus_ref[0]) * g_ref[0]).astype(BF16)


def _post(o, shared, hf, hb, z_ab, cb, seq_len, p):
    m, wc = o.shape[1], o.shape[2]
    wa, wb = hf.shape[1], cb.shape[1]
    dmix = wa + wb + wc
    tm = _pick(seq_len, (256, 128, 64, 32, 16))
    vec = lambda n: pl.BlockSpec((1, n), lambda i: (0, 0))
    return pl.pallas_call(
        functools.partial(_post_body, wa=wa, wb=wb),
        grid=(m // tm,),
        in_specs=[pl.BlockSpec((2, tm, wc), lambda i: (0, i, 0)),
                  pl.BlockSpec((1, tm, wc), lambda i: (3, i, 0)),
                  pl.BlockSpec((1, tm, wc), lambda i: (4, i, 0)),
                  pl.BlockSpec((tm, wa), lambda i: (i, 0)),
                  pl.BlockSpec((tm, wa), lambda i: (i, 0)),
                  pl.BlockSpec((tm, wa), lambda i: (i, 1)),
                  pl.BlockSpec((tm, wb), lambda i: (i, 0)),
                  vec(wb), vec(wb), vec(wc), vec(wc)],
        out_specs=pl.BlockSpec((tm, dmix), lambda i: (i, 0)),
        out_shape=jax.ShapeDtypeStruct((m, dmix), BF16),
        compiler_params=_params("parallel"),
        name="mix_post",
    )(o, shared, shared, hf, hb, z_ab, cb, p["conv_ln_g"], p["conv_ln_b"], p["rwkv_lnx_g"], p["rwkv_lnx_b"])


def _layer(x, h, bsz, seq_len, mods, nxt, p, h0_lru, s0_wkv, on_grid, alpha):
    wa = p["conv_a_w"].shape[1]
    z_ab = _matmul(h, p["w_in_ab"], F32, name="w_in_ab")
    z_c = _matmul(h, p["w_in_c"], F32, name="w_in_c")
    ab = _lru_gates(z_ab, seq_len, p)
    hf, hb, st_lru = _lru_scan(ab, h0_lru, bsz, seq_len)
    cb = _convb(z_ab, bsz, seq_len, 2 * wa, p, on_grid)
    shared, perdir = _rwkv_prep(z_c, seq_len, p)
    o, st_wkv = _wkv(shared, perdir, s0_wkv, bsz, seq_len)
    ycat = _post(o, shared, hf, hb, z_ab, cb, seq_len, p)
    y = _matmul(ycat, p["w_out"], F32, name="w_out")
    x, h2 = _ln_mod(x, seq_len, p["ln1_g"], p["ln1_b"], res=(y, mods["g1"]), mod=(mods["sc2"], mods["sh2"]),
                    alpha=alpha)
    f = _matmul(_matmul(h2, p["w_up"], BF16, relu2=True, name="ffn_up"), p["w_down"], F32, name="ffn_down")
    x, hn = _ln_mod(x, seq_len, p["ln2_g"], p["ln2_b"], res=(f, mods["g2"]), mod=nxt, alpha=alpha)
    return x, hn, st_lru, st_wkv


def _block_diag_pairs(w):
    two, nb, hd, _ = w.shape
    w = w.reshape(two, nb // 2, 2, hd, hd)
    z = jnp.zeros_like(w[:, :, 0])
    top = jnp.concatenate([w[:, :, 0], z], axis=-1)
    bot = jnp.concatenate([z, w[:, :, 1]], axis=-1)
    return jnp.concatenate([top, bot], axis=-2)


def _pad_cols(a, segs):
    out, c = [], 0
    for w, wp in segs:
        piece = a[..., c:c + w]
        if wp > w:
            piece = jnp.pad(piece, [(0, 0)] * (a.ndim - 1) + [(0, wp - w)])
        out.append(piece)
        c += w
    return jnp.concatenate(out, axis=-1)


def kernel(x_prompt, x_sample, state_lru, state_wkv, c, c_ctx, ln_in_g, ln_in_b, w_mod, b_mod, w_in, w_out, conv_a_w, conv_a_b, lru_ga_w, lru_ga_b, lru_gx_w, lru_gx_b, lru_lambda, conv_b_w, conv_b_b, conv_ln_g, conv_ln_b, ts_mu_prev, ts_mu_next, rwkv_w0, rwkv_w_up, rwkv_a0, rwkv_a_up, rwkv_g_up, rwkv_k_k, rwkv_k_a, rwkv_r_k, rwkv_lnx_g, rwkv_lnx_b, ln1_g, ln1_b, w_up, w_down, ln2_g, ln2_b):
    bp, tp, d = x_prompt.shape
    bl, tl, _ = x_sample.shape
    depth = w_in.shape[0]
    wa = conv_a_w.shape[-1]
    wb = conv_b_w.shape[-1]
    wc = rwkv_k_k.shape[-1]
    rw, ra, rg = rwkv_w_up.shape[2], rwkv_a_up.shape[2], rwkv_g_up.shape[1]
    rwp, rap = _round_up(rw, LANES), _round_up(ra, LANES)
    nheads = wc // HEAD
    alpha = (2 * depth) ** 0.25
    assert CHUNK == HEAD and tp % CHUNK == 0 and tl % CHUNK == 0 and tl % GRID_W == 0

    rows = _round_up(1 + bl, 2 * SUBLANES)
    cc = jnp.zeros((rows, d), F32).at[0].set(c_ctx).at[1:1 + bl].set(c)
    mod = _modulation(cc, w_mod, b_mod).reshape(depth, rows, 6, 1, d)

    def mods_of(l, lo, hi):
        names = ("sh1", "sc1", "g1", "sh2", "sc2", "g2")
        return {n: mod[l, lo:hi, j] for j, n in enumerate(names)}

    c_segs = [(3 * wc, 3 * wc), (rw, rwp), (ra, rap), (rg, rg)]
    nab = 2 * wa + 2 * wb
    layers = []
    for l in range(depth):
        layers.append({
            "w_in_ab": w_in[l, :, :nab].astype(BF16),
            "w_in_c": _pad_cols(w_in[l, :, nab:], c_segs).astype(BF16),
            "w_out": w_out[l].astype(BF16), "w_up": w_up[l].astype(BF16), "w_down": w_down[l].astype(BF16),
            "conv_a_w": conv_a_w[l], "conv_a_b": conv_a_b[l].reshape(1, wa),
            "lru_ga_w": _block_diag_pairs(lru_ga_w[l]).astype(BF16), "lru_ga_b": lru_ga_b[l].reshape(2, 1, wa),
            "lru_gx_w": _block_diag_pairs(lru_gx_w[l]).astype(BF16), "lru_gx_b": lru_gx_b[l].reshape(2, 1, wa),
            "lru_lambda": lru_lambda[l].reshape(2, 1, wa),
            "conv_b_w": conv_b_w[l], "conv_b_b": conv_b_b[l].reshape(1, wb),
            "conv_ln_g": conv_ln_g[l].reshape(1, wb), "conv_ln_b": conv_ln_b[l].reshape(1, wb),
            "ts_mu_prev": _pad_cols(ts_mu_prev[l].reshape(1, -1), c_segs),
            "ts_mu_next": _pad_cols(ts_mu_next[l].reshape(1, -1), c_segs),
            "rwkv_w0": rwkv_w0[l].reshape(2, 1, wc),
            "rwkv_w_up": jnp.pad(rwkv_w_up[l], ((0, 0), (0, rwp - rw), (0, 0))).astype(BF16),
            "rwkv_a0": rwkv_a0[l].reshape(2, 1, wc),
            "rwkv_a_up": jnp.pad(rwkv_a_up[l], ((0, 0), (0, rap - ra), (0, 0))).astype(BF16),
            "rwkv_g_up": rwkv_g_up[l].astype(BF16),
            "rwkv_k_k": rwkv_k_k[l].reshape(1, wc), "rwkv_k_a": rwkv_k_a[l].reshape(1, wc),
            "rwkv_r_k": rwkv_r_k[l].reshape(1, wc),
            "rwkv_lnx_g": rwkv_lnx_g[l].reshape(1, wc), "rwkv_lnx_b": rwkv_lnx_b[l].reshape(1, wc),
            "ln1_g": ln1_g[l], "ln1_b": ln1_b[l], "ln2_g": ln2_g[l], "ln2_b": ln2_b[l],
        })

    m0c, m0l = mods_of(0, 0, 1), mods_of(0, 1, 1 + bl)
    xp, hp = _ln_mod(x_prompt.reshape(bp * tp, d), tp, ln_in_g, ln_in_b, mod=(m0c["sc1"], m0c["sh1"]))
    xs, hs = _ln_mod(x_sample.reshape(bl * tl, d), tl, ln_in_g, ln_in_b, mod=(m0l["sc1"], m0l["sh1"]))
    h0_ctx = jnp.zeros((bp, 2, wa), F32)
    s0_ctx = jnp.zeros((bp, 2, nheads, HEAD, HEAD), F32)
    lru_states, wkv_states = [], []
    for l in range(depth):
        mc, ml = mods_of(l, 0, 1), mods_of(l, 1, 1 + bl)
        if l + 1 < depth:
            nc_, nl_ = mods_of(l + 1, 0, 1), mods_of(l + 1, 1, 1 + bl)
            nxt_c, nxt_l = (nc_["sc1"], nc_["sh1"]), (nl_["sc1"], nl_["sh1"])
        else:
            nxt_c = nxt_l = None
        xp, hp, st_lru, st_wkv = _layer(xp, hp, bp, tp, mc, nxt_c, layers[l], h0_ctx, s0_ctx, False, alpha)
        lru_states.append(st_lru)
        wkv_states.append(st_wkv)
        xs, hs, _, _ = _layer(xs, hs, bl, tl, ml, nxt_l, layers[l], state_lru[:, l].astype(F32),
                              state_wkv[:, l].astype(F32), True, alpha)
    return (xp.reshape(bp, tp, d), xs.reshape(bl, tl, d),
            jnp.stack(lru_states, axis=1), jnp.stack(wkv_states, axis=1))
```

```python
import functools
import math

import jax
import jax.numpy as jnp
from jax import lax
from jax.experimental import pallas as pl
from jax.experimental.pallas import tpu as pltpu

F32 = jnp.float32
BF16 = jnp.bfloat16
HI = lax.Precision.HIGHEST

LANES = 128
SUBLANES = 8
VMEM_LIMIT = 60 * 1024 * 1024
GRID_W = 64
HEAD = 64
CHUNK = 64
C_LRU = 8.0
LN_EPS = 1e-5
GN_EPS = 64e-5


def _pick(n, cands):
    for c in cands:
        if c <= n and n % c == 0:
            return c
    return n


def _round_up(n, m):
    return (n + m - 1) // m * m


def _params(*sem):
    return pltpu.CompilerParams(dimension_semantics=sem, vmem_limit_bytes=VMEM_LIMIT)


def _sigmoid(x):
    return 1.0 / (1.0 + jnp.exp(-x))


def _softplus(x):
    return jnp.maximum(x, 0.0) + jnp.log(1.0 + jnp.exp(-jnp.abs(x)))


def _gelu_tanh(x):
    return 0.5 * x * (1.0 + jnp.tanh(math.sqrt(2.0 / math.pi) * (x + 0.044715 * (x * x * x))))


def _head_ones():
    r = lax.broadcasted_iota(jnp.int32, (LANES, LANES), 0) // HEAD
    c = lax.broadcasted_iota(jnp.int32, (LANES, LANES), 1) // HEAD
    return (r == c).astype(F32)


def _head_sum(x, ones):
    parts = [jnp.dot(x[:, j * LANES:(j + 1) * LANES], ones, precision=HI, preferred_element_type=F32)
             for j in range(x.shape[1] // LANES)]
    return parts[0] if len(parts) == 1 else jnp.concatenate(parts, axis=1)


def _mod_body(c_ref, w_ref, b_ref, o_ref):
    c = c_ref[...]
    a = (c * _sigmoid(c)).astype(BF16)
    o_ref[0] = jnp.dot(a, w_ref[0].astype(BF16), preferred_element_type=F32) + b_ref[0]


def _modulation(cc, w_mod, b_mod):
    depth, d, n = w_mod.shape
    rows = cc.shape[0]
    tn = _pick(n, (512, 256, 128))
    return pl.pallas_call(
        _mod_body,
        grid=(depth, n // tn),
        in_specs=[pl.BlockSpec((rows, d), lambda l, j: (0, 0)),
                  pl.BlockSpec((1, d, tn), lambda l, j: (l, 0, j)),
                  pl.BlockSpec((1, 1, tn), lambda l, j: (l, 0, j))],
        out_specs=pl.BlockSpec((1, rows, tn), lambda l, j: (l, 0, j)),
        out_shape=jax.ShapeDtypeStruct((depth, rows, n), F32),
        compiler_params=_params("parallel", "parallel"),
        name="modulation",
    )(cc, w_mod, b_mod.reshape(depth, 1, n))


def _mm_body(a_ref, w_ref, o_ref, *scratch, nk, relu2):
    def finish(r):
        if relu2:
            r = jnp.square(jnp.maximum(r, 0.0))
        o_ref[...] = r.astype(o_ref.dtype)

    if nk == 1:
        finish(jnp.dot(a_ref[...], w_ref[...], preferred_element_type=F32))
        return
    acc_ref, = scratch
    k = pl.program_id(2)

    @pl.when(k == 0)
    def _():
        acc_ref[...] = jnp.zeros_like(acc_ref)

    acc_ref[...] += jnp.dot(a_ref[...], w_ref[...], preferred_element_type=F32)

    @pl.when(k == nk - 1)
    def _():
        finish(acc_ref[...])


def _matmul(a, w, out_dtype, relu2=False, name="matmul"):
    m, kdim = a.shape
    n = w.shape[1]
    tm = _pick(m, (1024, 512, 256, 128, 64, 32, 16))
    tn = _pick(n, (512, 256, 128))
    tk = _pick(kdim, (4096, 2048, 1024, 512, 256, 128))
    nk = kdim // tk
    return pl.pallas_call(
        functools.partial(_mm_body, nk=nk, relu2=relu2),
        grid=(m // tm, n // tn, nk),
        in_specs=[pl.BlockSpec((tm, tk), lambda i, j, k: (i, k)),
                  pl.BlockSpec((tk, tn), lambda i, j, k: (k, j))],
        out_specs=pl.BlockSpec((tm, tn), lambda i, j, k: (i, j)),
        out_shape=jax.ShapeDtypeStruct((m, n), out_dtype),
        scratch_shapes=[pltpu.VMEM((tm, tn), F32)] if nk > 1 else [],
        compiler_params=_params("parallel", "parallel", "arbitrary"),
        name=name,
    )(a, w)


def _ln_body(*refs, has_res, has_mod, alpha):
    it = iter(refs)
    x_ref = next(it)
    if has_res:
        y_ref, gate_ref = next(it), next(it)
    g_ref, b_ref = next(it), next(it)
    if has_mod:
        sc_ref, sh_ref = next(it), next(it)
    xo_ref = next(it)
    x = x_ref[...]
    if has_res:
        x = alpha * x + gate_ref[0] * y_ref[...]
    mu = jnp.mean(x, axis=-1, keepdims=True)
    xc = x - mu
    var = jnp.mean(xc * xc, axis=-1, keepdims=True)
    xn = xc * lax.rsqrt(var + LN_EPS) * g_ref[...] + b_ref[...]
    xo_ref[...] = xn
    if has_mod:
        h_ref = next(it)
        h_ref[...] = (xn * (1.0 + sc_ref[0]) + sh_ref[0]).astype(BF16)


def _ln_mod(x, seq_len, g, b, res=None, mod=None, alpha=1.0):
    m, d = x.shape
    tm = _pick(seq_len, (256, 128, 64, 32, 16))

    def per_seq(arr):
        if arr.shape[0] == 1:
            return pl.BlockSpec((1, 1, d), lambda i: (0, 0, 0))
        return pl.BlockSpec((1, 1, d), lambda i: ((i * tm) // seq_len, 0, 0))

    row = pl.BlockSpec((tm, d), lambda i: (i, 0))
    vec = pl.BlockSpec((1, d), lambda i: (0, 0))
    args, specs = [x], [row]
    if res is not None:
        args += [res[0], res[1]]
        specs += [row, per_seq(res[1])]
    args += [g.reshape(1, d), b.reshape(1, d)]
    specs += [vec, vec]
    if mod is not None:
        args += [mod[0], mod[1]]
        specs += [per_seq(mod[0]), per_seq(mod[1])]
    out_shape = [jax.ShapeDtypeStruct((m, d), F32)]
    out_specs = [row]
    if mod is not None:
        out_shape.append(jax.ShapeDtypeStruct((m, d), BF16))
        out_specs.append(row)
    out = pl.pallas_call(
        functools.partial(_ln_body, has_res=res is not None, has_mod=mod is not None, alpha=alpha),
        grid=(m // tm,),
        in_specs=specs,
        out_specs=out_specs,
        out_shape=out_shape,
        compiler_params=_params("parallel"),
        name="ln_mod",
    )(*args)
    return out if mod is not None else (out[0], None)


def _halo_specs(tm, width, col, m):
    nb8 = m // SUBLANES
    r8 = tm // SUBLANES
    return [pl.BlockSpec((tm, width), lambda i: (i, col)),
            pl.BlockSpec((SUBLANES, width), lambda i: (jnp.maximum(i * r8 - 1, 0), col)),
            pl.BlockSpec((SUBLANES, width), lambda i: (jnp.minimum((i + 1) * r8, nb8 - 1), col))]


def _fill_ext(ext_ref, z_ref, zp_ref, zn_ref, tm, seq_len):
    i = pl.program_id(0)
    first = (i * tm) % seq_len == 0
    last = ((i + 1) * tm) % seq_len == 0
    ext_ref[0:SUBLANES] = jnp.where(first, 0.0, zp_ref[...])
    ext_ref[SUBLANES:SUBLANES + tm] = z_ref[...]
    ext_ref[SUBLANES + tm:2 * SUBLANES + tm] = jnp.where(last, 0.0, zn_ref[...])


def _lru_gates_body(zx_ref, zp_ref, zn_ref, cw_ref, cb_ref, gaw_ref, gab_ref, gxw_ref, gxb_ref, lam_ref,
                    o_ref, ext_ref, *, tm, seq_len, nj, taps):
    _fill_ext(ext_ref, zx_ref, zp_ref, zn_ref, tm, seq_len)
    u = cb_ref[...] + cw_ref[0:1, :] * ext_ref[SUBLANES - 1:SUBLANES - 1 + tm, :]
    for k in range(1, taps):
        u = u + cw_ref[k:k + 1, :] * ext_ref[SUBLANES - 1 + k:SUBLANES - 1 + k + tm, :]
    for j in range(nj):
        sl = slice(j * LANES, (j + 1) * LANES)
        uj = u[:, sl]
        ub = uj.astype(BF16)
        for d in range(2):
            r = _sigmoid(jnp.dot(ub, gaw_ref[d, j], preferred_element_type=F32) + gab_ref[d][:, sl])
            ig = _sigmoid(jnp.dot(ub, gxw_ref[d, j], preferred_element_type=F32) + gxb_ref[d][:, sl])
            log_a = (-C_LRU) * r * _softplus(-lam_ref[d][:, sl])
            a = jnp.exp(log_a)
            bb = jnp.sqrt(-jnp.tanh(log_a) * (a * a + 1.0)) * (ig * uj)
            o_ref[2 * d, :, sl] = a
            o_ref[2 * d + 1, :, sl] = bb


def _lru_gates(z_ab, seq_len, p):
    m = z_ab.shape[0]
    wa = p["conv_a_w"].shape[1]
    taps = p["conv_a_w"].shape[0]
    nj = wa // LANES
    tm = _pick(seq_len, (256, 128, 64, 32, 16))
    full = lambda a: pl.BlockSpec(a.shape, lambda i: (0,) * a.ndim)
    consts = [p["conv_a_w"], p["conv_a_b"], p["lru_ga_w"], p["lru_ga_b"], p["lru_gx_w"], p["lru_gx_b"],
              p["lru_lambda"]]
    return pl.pallas_call(
        functools.partial(_lru_gates_body, tm=tm, seq_len=seq_len, nj=nj, taps=taps),
        grid=(m // tm,),
        in_specs=_halo_specs(tm, wa, 0, m) + [full(a) for a in consts],
        out_specs=pl.BlockSpec((4, tm, wa), lambda i: (0, i, 0)),
        out_shape=jax.ShapeDtypeStruct((4, m, wa), F32),
        scratch_shapes=[pltpu.VMEM((tm + 2 * SUBLANES, wa), F32)],
        compiler_params=_params("parallel"),
        name="lru_gates",
    )(z_ab, z_ab, z_ab, *consts)


def _lru_scan_body(af_ref, bf_ref, ab_ref, bb_ref, h0_ref, hf_ref, hb_ref, st_ref, carry_ref, *, tt, nt):
    i = pl.program_id(1)

    @pl.when(i == 0)
    def _():
        carry_ref[...] = h0_ref[0]

    def step(t, c):
        hf, hb = c
        hf = af_ref[0, 0, t] * hf + bf_ref[0, 0, t]
        hf_ref[0, t] = hf
        tb = tt - 1 - t
        hb = ab_ref[0, 0, tb] * hb + bb_ref[0, 0, tb]
        hb_ref[0, tb] = hb
        return hf, hb

    hf, hb = lax.fori_loop(0, tt, step, (carry_ref[0], carry_ref[1]), unroll=8)
    carry_ref[0] = hf
    carry_ref[1] = hb

    @pl.when(i == nt - 1)
    def _():
        st_ref[0, 0] = hf
        st_ref[0, 1] = hb


def _lru_scan(ab, h0, bsz, seq_len):
    wa = ab.shape[-1]
    s8 = wa // LANES
    ab5 = ab.reshape(4, bsz, seq_len, s8, LANES)
    h0 = h0.reshape(bsz, 2, s8, LANES)
    tt = _pick(seq_len, (256, 128, 64, 32, 16, 8))
    nt = seq_len // tt
    blk = (1, 1, tt, s8, LANES)
    oblk = (1, tt, s8, LANES)
    hf, hb, st = pl.pallas_call(
        functools.partial(_lru_scan_body, tt=tt, nt=nt),
        grid=(bsz, nt),
        in_specs=[pl.BlockSpec(blk, lambda b, i: (0, b, i, 0, 0)),
                  pl.BlockSpec(blk, lambda b, i: (1, b, i, 0, 0)),
                  pl.BlockSpec(blk, lambda b, i: (2, b, nt - 1 - i, 0, 0)),
                  pl.BlockSpec(blk, lambda b, i: (3, b, nt - 1 - i, 0, 0)),
                  pl.BlockSpec((1, 2, s8, LANES), lambda b, i: (b, 0, 0, 0))],
        out_specs=[pl.BlockSpec(oblk, lambda b, i: (b, i, 0, 0)),
                   pl.BlockSpec(oblk, lambda b, i: (b, nt - 1 - i, 0, 0)),
                   pl.BlockSpec((1, 2, s8, LANES), lambda b, i: (b, 0, 0, 0))],
        out_shape=[jax.ShapeDtypeStruct((bsz, seq_len, s8, LANES), F32),
                   jax.ShapeDtypeStruct((bsz, seq_len, s8, LANES), F32),
                   jax.ShapeDtypeStruct((bsz, 2, s8, LANES), F32)],
        scratch_shapes=[pltpu.VMEM((2, s8, LANES), F32)],
        compiler_params=_params("parallel", "arbitrary"),
        name="lru_scan",
    )(ab5, ab5, ab5, ab5, h0)
    m = bsz * seq_len
    return hf.reshape(m, wa), hb.reshape(m, wa), st.reshape(bsz, 2, wa)


def _convb_body(zv_ref, zg_ref, w_ref, b_ref, o_ref, pad_ref, *, seq_len, pad, stride, taps, rc):
    zeros = jnp.zeros((pad, LANES), F32)
    pad_ref[0:pad] = zeros
    pad_ref[pad + seq_len:2 * pad + seq_len] = zeros
    pad_ref[pad:pad + seq_len] = zv_ref[0] * _sigmoid(zg_ref[0])
    half = taps // 2

    def rows(r0):
        acc = jnp.broadcast_to(b_ref[...], (rc, LANES))
        for k in range(taps):
            acc = acc + w_ref[k:k + 1, :] * pad_ref[pl.ds(r0 + pad + (k - half) * stride, rc), :]
        o_ref[0, pl.ds(r0, rc), :] = acc

    nchunks = seq_len // rc
    if stride == 1:
        for c in range(nchunks):
            rows(c * rc)
    else:
        def chunk(c, carry):
            rows(pl.multiple_of(c * rc, rc))
            return carry
        lax.fori_loop(0, nchunks, chunk, 0)


def _convb(z_ab, bsz, seq_len, col0, p, on_grid):
    wb = p["conv_b_w"].shape[1]
    taps = p["conv_b_w"].shape[0]
    nj = wb // LANES
    stride = GRID_W if on_grid else 1
    pad = _round_up((taps // 2) * stride, SUBLANES * 2)
    rc = _pick(seq_len, (64, 32, 16, 8)) if stride == 1 else _pick(seq_len, (256, 128, 64))
    z3 = z_ab.reshape(bsz, seq_len, z_ab.shape[1])
    cv, cg = col0 // LANES, (col0 + wb) // LANES
    out = pl.pallas_call(
        functools.partial(_convb_body, seq_len=seq_len, pad=pad, stride=stride, taps=taps, rc=rc),
        grid=(bsz, nj),
        in_specs=[pl.BlockSpec((1, seq_len, LANES), lambda b, j: (b, 0, cv + j)),
                  pl.BlockSpec((1, seq_len, LANES), lambda b, j: (b, 0, cg + j)),
                  pl.BlockSpec((taps, LANES), lambda b, j: (0, j)),
                  pl.BlockSpec((1, LANES), lambda b, j: (0, j))],
        out_specs=pl.BlockSpec((1, seq_len, LANES), lambda b, j: (b, 0, j)),
        out_shape=jax.ShapeDtypeStruct((bsz, seq_len, wb), F32),
        scratch_shapes=[pltpu.VMEM((seq_len + 2 * pad, LANES), F32)],
        compiler_params=_params("parallel", "parallel"),
        name="convb_grid" if on_grid else "convb_seq",
    )(z3, z3, p["conv_b_w"], p["conv_b_b"])
    return out.reshape(bsz * seq_len, wb)


def _prep_body(z_ref, zp_ref, zn_ref, mup_ref, mun_ref, w0_ref, wup_ref, a0_ref, aup_ref, gup_ref,
               kk_ref, ka_ref, rk_ref, sh_ref, pd_ref, ext_ref, *, tm, seq_len, wc, rw, ra):
    _fill_ext(ext_ref, z_ref, zp_ref, zn_ref, tm, seq_len)
    ones = _head_ones()

    def shifted(c0, c1):
        z = ext_ref[SUBLANES:SUBLANES + tm, c0:c1]
        zp = ext_ref[SUBLANES - 1:SUBLANES - 1 + tm, c0:c1]
        zn = ext_ref[SUBLANES + 1:SUBLANES + 1 + tm, c0:c1]
        return z + mup_ref[:, c0:c1] * (zp - z) + mun_ref[:, c0:c1] * (zn - z)

    r = shifted(0, wc)
    k = shifted(wc, 2 * wc)
    v = shifted(2 * wc, 3 * wc)
    xw = shifted(3 * wc, 3 * wc + rw)
    xa = shifted(3 * wc + rw, 3 * wc + rw + ra)
    xg = shifted(3 * wc + rw + ra, z_ref.shape[1])

    sh_ref[0] = r
    sh_ref[1] = v
    kk = k * kk_ref[...]
    kap = kk / jnp.maximum(jnp.sqrt(_head_sum(kk * kk, ones)), 1e-12)
    sh_ref[2] = kap
    sh_ref[3] = jnp.dot(_sigmoid(xg).astype(BF16), gup_ref[...], preferred_element_type=F32)

    tw = jnp.tanh(xw).astype(BF16)
    xab = xa.astype(BF16)
    kdsum = None
    for d in range(2):
        x = w0_ref[d] + jnp.dot(tw, wup_ref[d], preferred_element_type=F32)
        pd_ref[d, 0] = (-math.exp(-0.5)) * _sigmoid(x)
        eta = _sigmoid(a0_ref[d] + jnp.dot(xab, aup_ref[d], preferred_element_type=F32))
        pd_ref[d, 1] = kap * eta
        kd = k * (1.0 + (eta - 1.0) * ka_ref[...])
        pd_ref[d, 2] = kd
        kdsum = kd if kdsum is None else kdsum + kd
    sh_ref[4] = _head_sum(r * kdsum * rk_ref[...], ones) * v


def _rwkv_prep(z_c, seq_len, p):
    m, wz = z_c.shape
    wc = p["rwkv_k_k"].shape[1]
    rw, ra = p["rwkv_w_up"].shape[1], p["rwkv_a_up"].shape[1]
    tm = _pick(seq_len, (128, 64, 32, 16))
    full = lambda a: pl.BlockSpec(a.shape, lambda i: (0,) * a.ndim)
    consts = [p["ts_mu_prev"], p["ts_mu_next"], p["rwkv_w0"], p["rwkv_w_up"], p["rwkv_a0"], p["rwkv_a_up"],
              p["rwkv_g_up"], p["rwkv_k_k"], p["rwkv_k_a"], p["rwkv_r_k"]]
    return pl.pallas_call(
        functools.partial(_prep_body, tm=tm, seq_len=seq_len, wc=wc, rw=rw, ra=ra),
        grid=(m // tm,),
        in_specs=_halo_specs(tm, wz, 0, m) + [full(a) for a in consts],
        out_specs=[pl.BlockSpec((5, tm, wc), lambda i: (0, i, 0)),
                   pl.BlockSpec((2, 3, tm, wc), lambda i: (0, 0, i, 0))],
        out_shape=[jax.ShapeDtypeStruct((5, m, wc), F32),
                   jax.ShapeDtypeStruct((2, 3, m, wc), F32)],
        scratch_shapes=[pltpu.VMEM((tm + 2 * SUBLANES, wz), F32)],
        compiler_params=_params("parallel"),
        name="rwkv_prep",
    )(z_c, z_c, z_c, *consts)


_NN = (((1,), (0,)), ((), ()))
_NT = (((1,), (1,)), ((), ()))
_TN = (((0,), (0,)), ((), ()))
WKV_GROUP = 16


def _split(x, pieces):
    hi = x.astype(BF16)
    if pieces == 1:
        return (hi,)
    return (hi, (x - hi.astype(F32)).astype(BF16))


def _mm(xs, ys, dims=_NN):
    ax_x, ax_y = dims[0][0][0], dims[0][1][0]
    if len(xs) == 1:
        x, y = xs[0], ys[0]
    else:
        x = jnp.concatenate([xs[0], xs[1], xs[0]], axis=ax_x)
        y = jnp.concatenate([ys[0], ys[0], ys[1]], axis=ax_y)
    return lax.dot_general(x, y, dims, preferred_element_type=F32)


def _wkv_body(r_ref, v_ref, kap_ref, lw_ref, ke_ref, kd_ref, s0_ref, o_ref, so_ref,
              s_ref, a_sc, b_sc, k_sc, r_sc, b2_sc, k2_sc, g_sc, *, nc, npairs, group, inv_pieces):
    c = CHUNK
    d = pl.program_id(1)
    ci = pl.program_id(2)

    @pl.when(ci == 0)
    def _():
        zero = jnp.zeros((HEAD, HEAD), F32)
        for p in range(npairs):
            top = jnp.concatenate([s0_ref[0, 0, 2 * p], zero], axis=1)
            bot = jnp.concatenate([zero, s0_ref[0, 0, 2 * p + 1]], axis=1)
            s_ref[p] = jnp.concatenate([top, bot], axis=0)

    row = lax.broadcasted_iota(jnp.int32, (c, 2 * c), 0)
    col = lax.broadcasted_iota(jnp.int32, (c, 2 * c), 1) % c
    fwd = d == 0
    ahead = jnp.where(fwd, row - col, col - row)
    strict2 = ahead > 0
    incl2 = ahead >= 0
    lw = lw_ref[0, 0]
    l1 = lw.astype(BF16)
    rem = lw - l1.astype(F32)
    l2 = rem.astype(BF16)
    l3 = (rem - l2.astype(F32)).astype(BF16)
    tri = incl2[:, :c].astype(F32).astype(BF16)
    cum = jnp.dot(jnp.concatenate([tri, tri, tri], axis=1), jnp.concatenate([l1, l2, l3], axis=0),
                  preferred_element_type=F32)
    tot = jnp.where(fwd, cum[c - 1:c, :], cum[0:1, :])
    kap = kap_ref[0]
    ke = ke_ref[0, 0]
    kd = kd_ref[0, 0]
    a_sc[...] = -kap * jnp.exp(cum - lw)
    em = jnp.exp(-cum)
    b_sc[...] = ke * em
    k_sc[...] = kd * em
    r_sc[...] = r_ref[0] * jnp.exp(cum)
    e2 = jnp.exp(tot - cum)
    b2_sc[...] = ke * e2
    k2_sc[...] = kd * e2
    g_sc[...] = jnp.broadcast_to(jnp.exp(tot), g_sc.shape)

    lane = lax.broadcasted_iota(jnp.int32, (1, LANES), 1)
    m0 = lane < HEAD
    rbd = lax.broadcasted_iota(jnp.int32, (LANES, LANES), 0) // HEAD
    cbd = lax.broadcasted_iota(jnp.int32, (LANES, LANES), 1) // HEAD
    bdmask = rbd == cbd
    m0w = jnp.concatenate([m0, m0], axis=1)
    levels = c.bit_length() - 1

    def stack(x):
        m = m0 if x.shape[1] == LANES else m0w
        return jnp.concatenate([jnp.where(m, x, 0.0), jnp.where(m, 0.0, x)], axis=0)

    def bdot(x, y, dims=_NN):
        return lax.dot_general(x.astype(BF16), y.astype(BF16), dims, preferred_element_type=F32)

    def process(sls, sidx):
        n = len(sls)
        mbd, xbd, lr, vst, sbd = [], [], [], [], []
        for q in range(n):
            sl = sls[q]
            lm = bdot(jnp.concatenate([a_sc[:, sl], r_sc[:, sl]], axis=0),
                      jnp.concatenate([stack(b_sc[:, sl]), stack(k_sc[:, sl])], axis=0), _NT)
            lak = jnp.where(strict2, lm[:c, 2 * c:], 0.0)
            lr.append(jnp.concatenate([jnp.where(incl2, lm[c:, :2 * c], 0.0),
                                       jnp.where(incl2, lm[c:, 2 * c:], 0.0)], axis=1).astype(BF16))
            vst.append(stack(v_ref[0, :, sl]).astype(BF16))
            sbd.append(s_ref[sidx[q]].astype(BF16))
            xbd.append(bdot(a_sc[:, sl], sbd[q], _NT) + bdot(lak, vst[q]))
            mbd.append(jnp.where(strict2, lm[:c, :2 * c], 0.0))
        for lvl in range(levels):
            for q in range(n):
                lhs = _split(mbd[q], inv_pieces)
                if lvl + 1 < levels:
                    y = _mm(lhs, _split(stack(jnp.concatenate([xbd[q], mbd[q]], axis=1)), inv_pieces))
                    xbd[q] = xbd[q] + y[:, :LANES]
                    mbd[q] = y[:, LANES:]
                else:
                    xbd[q] = xbd[q] + _mm(lhs, _split(stack(xbd[q]), inv_pieces))
        for q in range(n):
            sl = sls[q]
            u = xbd[q]
            o_ref[0, :, sl] = (bdot(r_sc[:, sl], sbd[q], _NT)
                               + bdot(lr[q], jnp.concatenate([stack(u).astype(BF16), vst[q]], axis=0)))
            upd = bdot(jnp.concatenate([u, v_ref[0, :, sl]], axis=0),
                       jnp.concatenate([b2_sc[:, sl], k2_sc[:, sl]], axis=0), _TN)
            s_ref[sidx[q]] = s_ref[sidx[q]] * g_sc[0:1, sl] + jnp.where(bdmask, upd, 0.0)

    if group >= npairs:
        process([slice(p * LANES, (p + 1) * LANES) for p in range(npairs)], list(range(npairs)))
    else:
        def body(g, carry):
            idx = [g * group + q for q in range(group)]
            process([pl.ds(pl.multiple_of(i * LANES, LANES), LANES) for i in idx], idx)
            return carry
        lax.fori_loop(0, npairs // group, body, 0)

    @pl.when(ci == nc - 1)
    def _():
        for p in range(npairs):
            sbd = s_ref[p]
            so_ref[0, 0, 2 * p] = sbd[:HEAD, :HEAD]
            so_ref[0, 0, 2 * p + 1] = sbd[HEAD:, HEAD:]


def _wkv(shared, perdir, s0, bsz, seq_len, state_is_output):
    m, wc = shared.shape[1], shared.shape[2]
    nheads = wc // HEAD
    npairs = nheads // 2
    c = CHUNK
    nc = seq_len // c

    def chunk(b, d, i):
        return b * nc + jnp.where(d == 0, i, nc - 1 - i)

    sh = lambda j: pl.BlockSpec((1, c, wc), lambda b, d, i: (j, chunk(b, d, i), 0))
    pd = lambda j: pl.BlockSpec((1, 1, c, wc), lambda b, d, i: (d, j, chunk(b, d, i), 0))
    st = pl.BlockSpec((1, 1, nheads, HEAD, HEAD), lambda b, d, i: (b, d, 0, 0, 0))
    cw = pltpu.VMEM((c, wc), F32)
    return pl.pallas_call(
        functools.partial(_wkv_body, nc=nc, npairs=npairs,
                          group=WKV_GROUP if npairs % WKV_GROUP == 0 else npairs,
                          inv_pieces=2 if state_is_output else 1),
        grid=(bsz, 2, nc),
        in_specs=[sh(0), sh(1), sh(2), pd(0), pd(1), pd(2), st],
        out_specs=[pl.BlockSpec((1, c, wc), lambda b, d, i: (d, chunk(b, d, i), 0)), st],
        out_shape=[jax.ShapeDtypeStruct((2, m, wc), F32),
                   jax.ShapeDtypeStruct((bsz, 2, nheads, HEAD, HEAD), F32)],
        scratch_shapes=[pltpu.VMEM((npairs, LANES, LANES), F32), cw, cw, cw, cw, cw, cw,
                        pltpu.VMEM((SUBLANES, wc), F32)],
        compiler_params=_params("parallel", "parallel", "arbitrary"),
        name="wkv_scan",
    )(shared, shared, shared, perdir, perdir, perdir, s0)


def _post_body(o_ref, g_ref, bonus_ref, hf_ref, hb_ref, zg_ref, cb_ref, lng_ref, lnb_ref, xg_ref, xb_ref,
               y_ref, *, wa, wb):
    ones = _head_ones()
    y_ref[:, 0:wa] = ((hf_ref[...] + hb_ref[...]) * _gelu_tanh(zg_ref[...])).astype(BF16)

    u = cb_ref[...]
    mu = jnp.mean(u, axis=-1, keepdims=True)
    uc = u - mu
    var = jnp.mean(uc * uc, axis=-1, keepdims=True)
    un = uc * lax.rsqrt(var + LN_EPS) * lng_ref[...] + lnb_ref[...]
    y_ref[:, wa:wa + wb] = (un * _sigmoid(un)).astype(BF16)

    o = o_ref[0] + o_ref[1]
    inv = 1.0 / HEAD
    omu = _head_sum(o, ones) * inv
    oc = o - omu
    ovar = _head_sum(oc * oc, ones) * inv
    on = oc * lax.rsqrt(ovar + GN_EPS) * xg_ref[...] + xb_ref[...]
    y_ref[:, wa + wb:] = ((on + bonus_ref[0]) * g_ref[0]).astype(BF16)


def _post(o, shared, hf, hb, z_ab, cb, seq_len, p):
    m, wc = o.shape[1], o.shape[2]
    wa, wb = hf.shape[1], cb.shape[1]
    dmix = wa + wb + wc
    tm = _pick(seq_len, (256, 128, 64, 32, 16))
    vec = lambda n: pl.BlockSpec((1, n), lambda i: (0, 0))
    return pl.pallas_call(
        functools.partial(_post_body, wa=wa, wb=wb),
        grid=(m // tm,),
        in_specs=[pl.BlockSpec((2, tm, wc), lambda i: (0, i, 0)),
                  pl.BlockSpec((1, tm, wc), lambda i: (3, i, 0)),
                  pl.BlockSpec((1, tm, wc), lambda i: (4, i, 0)),
                  pl.BlockSpec((tm, wa), lambda i: (i, 0)),
                  pl.BlockSpec((tm, wa), lambda i: (i, 0)),
                  pl.BlockSpec((tm, wa), lambda i: (i, 1)),
                  pl.BlockSpec((tm, wb), lambda i: (i, 0)),
                  vec(wb), vec(wb), vec(wc), vec(wc)],
        out_specs=pl.BlockSpec((tm, dmix), lambda i: (i, 0)),
        out_shape=jax.ShapeDtypeStruct((m, dmix), BF16),
        compiler_params=_params("parallel"),
        name="mix_post",
    )(o, shared, shared, hf, hb, z_ab, cb, p["conv_ln_g"], p["conv_ln_b"], p["rwkv_lnx_g"], p["rwkv_lnx_b"])


def _layer(x, h, bsz, seq_len, mods, nxt, p, h0_lru, s0_wkv, on_grid, keep_state, alpha):
    wa = p["conv_a_w"].shape[1]
    z_ab = _matmul(h, p["w_in_ab"], F32, name="w_in_ab")
    z_c = _matmul(h, p["w_in_c"], F32, name="w_in_c")
    ab = _lru_gates(z_ab, seq_len, p)
    hf, hb, st_lru = _lru_scan(ab, h0_lru, bsz, seq_len)
    cb = _convb(z_ab, bsz, seq_len, 2 * wa, p, on_grid)
    shared, perdir = _rwkv_prep(z_c, seq_len, p)
    o, st_wkv = _wkv(shared, perdir, s0_wkv, bsz, seq_len, state_is_output=keep_state)
    ycat = _post(o, shared, hf, hb, z_ab, cb, seq_len, p)
    y = _matmul(ycat, p["w_out"], F32, name="w_out")
    x, h2 = _ln_mod(x, seq_len, p["ln1_g"], p["ln1_b"], res=(y, mods["g1"]), mod=(mods["sc2"], mods["sh2"]),
                    alpha=alpha)
    f = _matmul(_matmul(h2, p["w_up"], BF16, relu2=True, name="ffn_up"), p["w_down"], F32, name="ffn_down")
    x, hn = _ln_mod(x, seq_len, p["ln2_g"], p["ln2_b"], res=(f, mods["g2"]), mod=nxt, alpha=alpha)
    return x, hn, st_lru, st_wkv


def _block_diag_pairs(w):
    two, nb, hd, _ = w.shape
    w = w.reshape(two, nb // 2, 2, hd, hd)
    z = jnp.zeros_like(w[:, :, 0])
    top = jnp.concatenate([w[:, :, 0], z], axis=-1)
    bot = jnp.concatenate([z, w[:, :, 1]], axis=-1)
    return jnp.concatenate([top, bot], axis=-2)


def _pad_cols(a, segs):
    out, c = [], 0
    for w, wp in segs:
        piece = a[..., c:c + w]
        if wp > w:
            piece = jnp.pad(piece, [(0, 0)] * (a.ndim - 1) + [(0, wp - w)])
        out.append(piece)
        c += w
    return jnp.concatenate(out, axis=-1)


def kernel(x_prompt, x_sample, state_lru, state_wkv, c, c_ctx, ln_in_g, ln_in_b, w_mod, b_mod, w_in, w_out, conv_a_w, conv_a_b, lru_ga_w, lru_ga_b, lru_gx_w, lru_gx_b, lru_lambda, conv_b_w, conv_b_b, conv_ln_g, conv_ln_b, ts_mu_prev, ts_mu_next, rwkv_w0, rwkv_w_up, rwkv_a0, rwkv_a_up, rwkv_g_up, rwkv_k_k, rwkv_k_a, rwkv_r_k, rwkv_lnx_g, rwkv_lnx_b, ln1_g, ln1_b, w_up, w_down, ln2_g, ln2_b):
    bp, tp, d = x_prompt.shape
    bl, tl, _ = x_sample.shape
    depth = w_in.shape[0]
    wa = conv_a_w.shape[-1]
    wb = conv_b_w.shape[-1]
    wc = rwkv_k_k.shape[-1]
    rw, ra, rg = rwkv_w_up.shape[2], rwkv_a_up.shape[2], rwkv_g_up.shape[1]
    rwp, rap = _round_up(rw, LANES), _round_up(ra, LANES)
    nheads = wc // HEAD
    alpha = (2 * depth) ** 0.25
    assert CHUNK == HEAD and tp % CHUNK == 0 and tl % CHUNK == 0 and tl % GRID_W == 0

    rows = _round_up(1 + bl, 2 * SUBLANES)
    cc = jnp.zeros((rows, d), F32).at[0].set(c_ctx).at[1:1 + bl].set(c)
    mod = _modulation(cc, w_mod, b_mod).reshape(depth, rows, 6, 1, d)

    def mods_of(l, lo, hi):
        names = ("sh1", "sc1", "g1", "sh2", "sc2", "g2")
        return {n: mod[l, lo:hi, j] for j, n in enumerate(names)}

    c_segs = [(3 * wc, 3 * wc), (rw, rwp), (ra, rap), (rg, rg)]
    nab = 2 * wa + 2 * wb
    layers = []
    for l in range(depth):
        layers.append({
            "w_in_ab": w_in[l, :, :nab].astype(BF16),
            "w_in_c": _pad_cols(w_in[l, :, nab:], c_segs).astype(BF16),
            "w_out": w_out[l].astype(BF16), "w_up": w_up[l].astype(BF16), "w_down": w_down[l].astype(BF16),
            "conv_a_w": conv_a_w[l], "conv_a_b": conv_a_b[l].reshape(1, wa),
            "lru_ga_w": _block_diag_pairs(lru_ga_w[l]).astype(BF16), "lru_ga_b": lru_ga_b[l].reshape(2, 1, wa),
            "lru_gx_w": _block_diag_pairs(lru_gx_w[l]).astype(BF16), "lru_gx_b": lru_gx_b[l].reshape(2, 1, wa),
            "lru_lambda": lru_lambda[l].reshape(2, 1, wa),
            "conv_b_w": conv_b_w[l], "conv_b_b": conv_b_b[l].reshape(1, wb),
            "conv_ln_g": conv_ln_g[l].reshape(1, wb), "conv_ln_b": conv_ln_b[l].reshape(1, wb),
            "ts_mu_prev": _pad_cols(ts_mu_prev[l].reshape(1, -1), c_segs),
            "ts_mu_next": _pad_cols(ts_mu_next[l].reshape(1, -1), c_segs),
            "rwkv_w0": rwkv_w0[l].reshape(2, 1, wc),
            "rwkv_w_up": jnp.pad(rwkv_w_up[l], ((0, 0), (0, rwp - rw), (0, 0))).astype(BF16),
            "rwkv_a0": rwkv_a0[l].reshape(2, 1, wc),
            "rwkv_a_up": jnp.pad(rwkv_a_up[l], ((0, 0), (0, rap - ra), (0, 0))).astype(BF16),
            "rwkv_g_up": rwkv_g_up[l].astype(BF16),
            "rwkv_k_k": rwkv_k_k[l].reshape(1, wc), "rwkv_k_a": rwkv_k_a[l].reshape(1, wc),
            "rwkv_r_k": rwkv_r_k[l].reshape(1, wc),
            "rwkv_lnx_g": rwkv_lnx_g[l].reshape(1, wc), "rwkv_lnx_b": rwkv_lnx_b[l].reshape(1, wc),
            "ln1_g": ln1_g[l], "ln1_b": ln1_b[l], "ln2_g": ln2_g[l], "ln2_b": ln2_b[l],
        })

    m0c, m0l = mods_of(0, 0, 1), mods_of(0, 1, 1 + bl)
    xp, hp = _ln_mod(x_prompt.reshape(bp * tp, d), tp, ln_in_g, ln_in_b, mod=(m0c["sc1"], m0c["sh1"]))
    xs, hs = _ln_mod(x_sample.reshape(bl * tl, d), tl, ln_in_g, ln_in_b, mod=(m0l["sc1"], m0l["sh1"]))
    h0_ctx = jnp.zeros((bp, 2, wa), F32)
    s0_ctx = jnp.zeros((bp, 2, nheads, HEAD, HEAD), F32)
    lru_states, wkv_states = [], []
    for l in range(depth):
        mc, ml = mods_of(l, 0, 1), mods_of(l, 1, 1 + bl)
        if l + 1 < depth:
            nc_, nl_ = mods_of(l + 1, 0, 1), mods_of(l + 1, 1, 1 + bl)
            nxt_c, nxt_l = (nc_["sc1"], nc_["sh1"]), (nl_["sc1"], nl_["sh1"])
        else:
            nxt_c = nxt_l = None
        xp, hp, st_lru, st_wkv = _layer(xp, hp, bp, tp, mc, nxt_c, layers[l], h0_ctx, s0_ctx, False, True, alpha)
        lru_states.append(st_lru)
        wkv_states.append(st_wkv)
        xs, hs, _, _ = _layer(xs, hs, bl, tl, ml, nxt_l, layers[l], state_lru[:, l].astype(F32),
                              state_wkv[:, l].astype(F32), True, False, alpha)
    return (xp.reshape(bp, tp, d), xs.reshape(bl, tl, d),
            jnp.stack(lru_states, axis=1), jnp.stack(wkv_states, axis=1))
```

```python
import functools
import math

import jax
import jax.numpy as jnp
from jax import lax
from jax.experimental import pallas as pl
from jax.experimental.pallas import tpu as pltpu

F32 = jnp.float32
BF16 = jnp.bfloat16
HI = lax.Precision.HIGHEST

LANES = 128
SUBLANES = 8
VMEM_LIMIT = 60 * 1024 * 1024
GRID_W = 64
HEAD = 64
CHUNK = 64
C_LRU = 8.0
LN_EPS = 1e-5
GN_EPS = 64e-5


def _pick(n, cands):
    for c in cands:
        if c <= n and n % c == 0:
            return c
    return n


def _round_up(n, m):
    return (n + m - 1) // m * m


def _params(*sem):
    return pltpu.CompilerParams(dimension_semantics=sem, vmem_limit_bytes=VMEM_LIMIT)


def _sigmoid(x):
    return 0.5 * jnp.tanh(0.5 * x) + 0.5


def _softplus(x):
    return jnp.maximum(x, 0.0) + jnp.log(1.0 + jnp.exp(-jnp.abs(x)))


def _gelu_tanh(x):
    return 0.5 * x * (1.0 + jnp.tanh(math.sqrt(2.0 / math.pi) * (x + 0.044715 * (x * x * x))))


def _head_ones():
    r = lax.broadcasted_iota(jnp.int32, (LANES, LANES), 0) // HEAD
    c = lax.broadcasted_iota(jnp.int32, (LANES, LANES), 1) // HEAD
    return (r == c).astype(F32)


def _head_sum(x, ones):
    parts = [jnp.dot(x[:, j * LANES:(j + 1) * LANES], ones, precision=HI, preferred_element_type=F32)
             for j in range(x.shape[1] // LANES)]
    return parts[0] if len(parts) == 1 else jnp.concatenate(parts, axis=1)


def _mod_body(c_ref, w_ref, b_ref, o_ref):
    c = c_ref[...]
    a = (c * _sigmoid(c)).astype(BF16)
    o_ref[0] = jnp.dot(a, w_ref[0].astype(BF16), preferred_element_type=F32) + b_ref[0]


def _modulation(cc, w_mod, b_mod):
    depth, d, n = w_mod.shape
    rows = cc.shape[0]
    tn = _pick(n, (512, 256, 128))
    return pl.pallas_call(
        _mod_body,
        grid=(depth, n // tn),
        in_specs=[pl.BlockSpec((rows, d), lambda l, j: (0, 0)),
                  pl.BlockSpec((1, d, tn), lambda l, j: (l, 0, j)),
                  pl.BlockSpec((1, 1, tn), lambda l, j: (l, 0, j))],
        out_specs=pl.BlockSpec((1, rows, tn), lambda l, j: (l, 0, j)),
        out_shape=jax.ShapeDtypeStruct((depth, rows, n), F32),
        compiler_params=_params("parallel", "parallel"),
        name="modulation",
    )(cc, w_mod, b_mod.reshape(depth, 1, n))


def _mm_body(a_ref, w_ref, *rest, nk, relu2, alpha):
    if alpha is not None:
        x_ref, gate_ref, o_ref, *scratch = rest
    else:
        o_ref, *scratch = rest

    def finish(r):
        if relu2:
            r = jnp.square(jnp.maximum(r, 0.0))
        if alpha is not None:
            r = alpha * x_ref[...] + gate_ref[0] * r
        o_ref[...] = r.astype(o_ref.dtype)

    if nk == 1:
        finish(jnp.dot(a_ref[...], w_ref[...], preferred_element_type=F32))
        return
    acc_ref, = scratch
    k = pl.program_id(2)

    @pl.when(k == 0)
    def _():
        acc_ref[...] = jnp.zeros_like(acc_ref)

    acc_ref[...] += jnp.dot(a_ref[...], w_ref[...], preferred_element_type=F32)

    @pl.when(k == nk - 1)
    def _():
        finish(acc_ref[...])


def _matmul(a, w, layer, out_dtype, relu2=False, res=None, name="matmul"):
    m, kdim = a.shape
    n = w.shape[2]
    per_seq = res is not None and res[1].shape[0] > 1
    seq_len = res[2] if per_seq else m
    tm = _pick(seq_len, (1024, 512, 256, 128, 64, 32, 16))
    tn = _pick(n, (512, 256, 128))
    tk = _pick(kdim, (4096, 2048, 1024, 512, 256, 128))
    nk = kdim // tk
    args = [a, w]
    specs = [pl.BlockSpec((tm, tk), lambda i, j, k: (i, k)),
             pl.BlockSpec((None, tk, tn), lambda i, j, k: (layer, k, j))]
    alpha = None
    if res is not None:
        x, gate, _, alpha = res
        args += [x, gate]
        specs += [pl.BlockSpec((tm, tn), lambda i, j, k: (i, j)),
                  pl.BlockSpec((1, 1, tn), lambda i, j, k: ((i * tm) // seq_len if per_seq else 0, 0, j))]
    return pl.pallas_call(
        functools.partial(_mm_body, nk=nk, relu2=relu2, alpha=alpha),
        grid=(m // tm, n // tn, nk),
        in_specs=specs,
        out_specs=pl.BlockSpec((tm, tn), lambda i, j, k: (i, j)),
        out_shape=jax.ShapeDtypeStruct((m, n), out_dtype),
        scratch_shapes=[pltpu.VMEM((tm, tn), F32)] if nk > 1 else [],
        compiler_params=_params("parallel", "parallel", "arbitrary"),
        name=name,
    )(*args)


def _ln_body(*refs, has_mod):
    it = iter(refs)
    x_ref, g_ref, b_ref = next(it), next(it), next(it)
    if has_mod:
        sc_ref, sh_ref = next(it), next(it)
    xo_ref = next(it)
    x = x_ref[...]
    mu = jnp.mean(x, axis=-1, keepdims=True)
    xc = x - mu
    var = jnp.mean(xc * xc, axis=-1, keepdims=True)
    xn = xc * lax.rsqrt(var + LN_EPS) * g_ref[...] + b_ref[...]
    xo_ref[...] = xn
    if has_mod:
        h_ref = next(it)
        h_ref[...] = (xn * (1.0 + sc_ref[0]) + sh_ref[0]).astype(BF16)


def _ln_mod(x, seq_len, g, b, mod=None):
    m, d = x.shape
    tm = _pick(seq_len, (256, 128, 64, 32, 16))

    def per_seq(arr):
        if arr.shape[0] == 1:
            return pl.BlockSpec((1, 1, d), lambda i: (0, 0, 0))
        return pl.BlockSpec((1, 1, d), lambda i: ((i * tm) // seq_len, 0, 0))

    row = pl.BlockSpec((tm, d), lambda i: (i, 0))
    vec = pl.BlockSpec((1, d), lambda i: (0, 0))
    args, specs = [x, g.reshape(1, d), b.reshape(1, d)], [row, vec, vec]
    if mod is not None:
        args += [mod[0], mod[1]]
        specs += [per_seq(mod[0]), per_seq(mod[1])]
    out_shape = [jax.ShapeDtypeStruct((m, d), F32)]
    out_specs = [row]
    if mod is not None:
        out_shape.append(jax.ShapeDtypeStruct((m, d), BF16))
        out_specs.append(row)
    out = pl.pallas_call(
        functools.partial(_ln_body, has_mod=mod is not None),
        grid=(m // tm,),
        in_specs=specs,
        out_specs=out_specs,
        out_shape=out_shape,
        compiler_params=_params("parallel"),
        name="ln_mod",
    )(*args)
    return out if mod is not None else (out[0], None)


def _halo_specs(tm, width, col, m):
    nb8 = m // SUBLANES
    r8 = tm // SUBLANES
    return [pl.BlockSpec((tm, width), lambda i: (i, col)),
            pl.BlockSpec((SUBLANES, width), lambda i: (jnp.maximum(i * r8 - 1, 0), col)),
            pl.BlockSpec((SUBLANES, width), lambda i: (jnp.minimum((i + 1) * r8, nb8 - 1), col))]


def _fill_ext(ext_ref, z_ref, zp_ref, zn_ref, tm, seq_len):
    i = pl.program_id(0)
    first = (i * tm) % seq_len == 0
    last = ((i + 1) * tm) % seq_len == 0
    ext_ref[0:SUBLANES] = jnp.where(first, 0.0, zp_ref[...])
    ext_ref[SUBLANES:SUBLANES + tm] = z_ref[...]
    ext_ref[SUBLANES + tm:2 * SUBLANES + tm] = jnp.where(last, 0.0, zn_ref[...])


def _lru_gates_body(zx_ref, zp_ref, zn_ref, cw_ref, cb_ref, gaw_ref, gab_ref, gxw_ref, gxb_ref, lam_ref,
                    o_ref, ext_ref, *, tm, seq_len, nj, taps):
    _fill_ext(ext_ref, zx_ref, zp_ref, zn_ref, tm, seq_len)
    u = cb_ref[...] + cw_ref[0:1, :] * ext_ref[SUBLANES - 1:SUBLANES - 1 + tm, :]
    for k in range(1, taps):
        u = u + cw_ref[k:k + 1, :] * ext_ref[SUBLANES - 1 + k:SUBLANES - 1 + k + tm, :]
    for j in range(nj):
        sl = slice(j * LANES, (j + 1) * LANES)
        uj = u[:, sl]
        ub = uj.astype(BF16)
        for d in range(2):
            r = _sigmoid(jnp.dot(ub, gaw_ref[d, j], preferred_element_type=F32) + gab_ref[d][:, sl])
            ig = _sigmoid(jnp.dot(ub, gxw_ref[d, j], preferred_element_type=F32) + gxb_ref[d][:, sl])
            log_a = (-C_LRU) * r * _softplus(-lam_ref[d][:, sl])
            a = jnp.exp(log_a)
            bb = jnp.sqrt(-jnp.tanh(log_a) * (a * a + 1.0)) * (ig * uj)
            o_ref[2 * d, :, sl] = a
            o_ref[2 * d + 1, :, sl] = bb


def _lru_gates(z_ab, seq_len, p):
    m = z_ab.shape[0]
    wa = p["conv_a_w"].shape[1]
    taps = p["conv_a_w"].shape[0]
    nj = wa // LANES
    tm = _pick(seq_len, (256, 128, 64, 32, 16))
    full = lambda a: pl.BlockSpec(a.shape, lambda i: (0,) * a.ndim)
    consts = [p["conv_a_w"], p["conv_a_b"], p["lru_ga_w"], p["lru_ga_b"], p["lru_gx_w"], p["lru_gx_b"],
              p["lru_lambda"]]
    return pl.pallas_call(
        functools.partial(_lru_gates_body, tm=tm, seq_len=seq_len, nj=nj, taps=taps),
        grid=(m // tm,),
        in_specs=_halo_specs(tm, wa, 0, m) + [full(a) for a in consts],
        out_specs=pl.BlockSpec((4, tm, wa), lambda i: (0, i, 0)),
        out_shape=jax.ShapeDtypeStruct((4, m, wa), F32),
        scratch_shapes=[pltpu.VMEM((tm + 2 * SUBLANES, wa), F32)],
        compiler_params=_params("parallel"),
        name="lru_gates",
    )(z_ab, z_ab, z_ab, *consts)


def _lru_scan_body(af_ref, bf_ref, ab_ref, bb_ref, h0_ref, hf_ref, hb_ref, st_ref, carry_ref, *, tt, nt):
    i = pl.program_id(1)

    @pl.when(i == 0)
    def _():
        carry_ref[...] = h0_ref[0]

    def step(t, c):
        hf, hb = c
        hf = af_ref[0, 0, t] * hf + bf_ref[0, 0, t]
        hf_ref[0, t] = hf
        tb = tt - 1 - t
        hb = ab_ref[0, 0, tb] * hb + bb_ref[0, 0, tb]
        hb_ref[0, tb] = hb
        return hf, hb

    hf, hb = lax.fori_loop(0, tt, step, (carry_ref[0], carry_ref[1]), unroll=8)
    carry_ref[0] = hf
    carry_ref[1] = hb

    @pl.when(i == nt - 1)
    def _():
        st_ref[0, 0] = hf
        st_ref[0, 1] = hb


def _lru_scan(ab, h0, bsz, seq_len):
    wa = ab.shape[-1]
    s8 = wa // LANES
    ab5 = ab.reshape(4, bsz, seq_len, s8, LANES)
    h0 = h0.reshape(bsz, 2, s8, LANES)
    tt = _pick(seq_len, (256, 128, 64, 32, 16, 8))
    nt = seq_len // tt
    blk = (1, 1, tt, s8, LANES)
    oblk = (1, tt, s8, LANES)
    hf, hb, st = pl.pallas_call(
        functools.partial(_lru_scan_body, tt=tt, nt=nt),
        grid=(bsz, nt),
        in_specs=[pl.BlockSpec(blk, lambda b, i: (0, b, i, 0, 0)),
                  pl.BlockSpec(blk, lambda b, i: (1, b, i, 0, 0)),
                  pl.BlockSpec(blk, lambda b, i: (2, b, nt - 1 - i, 0, 0)),
                  pl.BlockSpec(blk, lambda b, i: (3, b, nt - 1 - i, 0, 0)),
                  pl.BlockSpec((1, 2, s8, LANES), lambda b, i: (b, 0, 0, 0))],
        out_specs=[pl.BlockSpec(oblk, lambda b, i: (b, i, 0, 0)),
                   pl.BlockSpec(oblk, lambda b, i: (b, nt - 1 - i, 0, 0)),
                   pl.BlockSpec((1, 2, s8, LANES), lambda b, i: (b, 0, 0, 0))],
        out_shape=[jax.ShapeDtypeStruct((bsz, seq_len, s8, LANES), F32),
                   jax.ShapeDtypeStruct((bsz, seq_len, s8, LANES), F32),
                   jax.ShapeDtypeStruct((bsz, 2, s8, LANES), F32)],
        scratch_shapes=[pltpu.VMEM((2, s8, LANES), F32)],
        compiler_params=_params("parallel", "arbitrary"),
        name="lru_scan",
    )(ab5, ab5, ab5, ab5, h0)
    m = bsz * seq_len
    return hf.reshape(m, wa), hb.reshape(m, wa), st.reshape(bsz, 2, wa)


def _convb_body(zv_ref, zg_ref, w_ref, b_ref, o_ref, pad_ref, *, nseq, seq_len, pad, stride, taps, rc):
    zeros = jnp.zeros((pad, LANES), F32)
    half = taps // 2
    nchunks = seq_len // rc
    for s in range(nseq):
        pad_ref[s, 0:pad] = zeros
        pad_ref[s, pad + seq_len:2 * pad + seq_len] = zeros
        pad_ref[s, pad:pad + seq_len] = zv_ref[s] * _sigmoid(zg_ref[s])

        def rows(r0, s=s):
            acc = jnp.broadcast_to(b_ref[...], (rc, LANES))
            for k in range(taps):
                acc = acc + w_ref[k:k + 1, :] * pad_ref[s, pl.ds(r0 + pad + (k - half) * stride, rc), :]
            o_ref[s, pl.ds(r0, rc), :] = acc

        if stride == 1:
            for c in range(nchunks):
                rows(c * rc)
        else:
            def chunk(c, carry, rows=rows):
                rows(pl.multiple_of(c * rc, rc))
                return carry
            lax.fori_loop(0, nchunks, chunk, 0)


def _convb(z_ab, bsz, seq_len, col0, p, on_grid):
    wb = p["conv_b_w"].shape[1]
    taps = p["conv_b_w"].shape[0]
    nj = wb // LANES
    stride = GRID_W if on_grid else 1
    pad = _round_up((taps // 2) * stride, SUBLANES * 2)
    rc = _pick(seq_len, (64, 32, 16, 8)) if stride == 1 else _pick(seq_len, (256, 128, 64))
    z3 = z_ab.reshape(bsz, seq_len, z_ab.shape[1])
    cv, cg = col0 // LANES, (col0 + wb) // LANES
    nseq = 1 if on_grid else _pick(bsz, (4, 2))
    out = pl.pallas_call(
        functools.partial(_convb_body, nseq=nseq, seq_len=seq_len, pad=pad, stride=stride, taps=taps, rc=rc),
        grid=(bsz // nseq, nj),
        in_specs=[pl.BlockSpec((nseq, seq_len, LANES), lambda b, j: (b, 0, cv + j)),
                  pl.BlockSpec((nseq, seq_len, LANES), lambda b, j: (b, 0, cg + j)),
                  pl.BlockSpec((taps, LANES), lambda b, j: (0, j)),
                  pl.BlockSpec((1, LANES), lambda b, j: (0, j))],
        out_specs=pl.BlockSpec((nseq, seq_len, LANES), lambda b, j: (b, 0, j)),
        out_shape=jax.ShapeDtypeStruct((bsz, seq_len, wb), F32),
        scratch_shapes=[pltpu.VMEM((nseq, seq_len + 2 * pad, LANES), F32)],
        compiler_params=_params("parallel", "parallel"),
        name="convb_grid" if on_grid else "convb_seq",
    )(z3, z3, p["conv_b_w"], p["conv_b_b"])
    return out.reshape(bsz * seq_len, wb)


def _prep_body(z_ref, zp_ref, zn_ref, mup_ref, mun_ref, w0_ref, wup_ref, a0_ref, aup_ref, gup_ref,
               kk_ref, ka_ref, rk_ref, sh_ref, lw_ref, pd_ref, ext_ref, *, tm, seq_len, wc, rw, ra):
    _fill_ext(ext_ref, z_ref, zp_ref, zn_ref, tm, seq_len)
    ones = _head_ones()

    def shifted(c0, c1):
        z = ext_ref[SUBLANES:SUBLANES + tm, c0:c1]
        zp = ext_ref[SUBLANES - 1:SUBLANES - 1 + tm, c0:c1]
        zn = ext_ref[SUBLANES + 1:SUBLANES + 1 + tm, c0:c1]
        return z + mup_ref[:, c0:c1] * (zp - z) + mun_ref[:, c0:c1] * (zn - z)

    r = shifted(0, wc)
    k = shifted(wc, 2 * wc)
    v = shifted(2 * wc, 3 * wc)
    xw = shifted(3 * wc, 3 * wc + rw)
    xa = shifted(3 * wc + rw, 3 * wc + rw + ra)
    xg = shifted(3 * wc + rw + ra, z_ref.shape[1])

    sh_ref[0] = r.astype(BF16)
    sh_ref[1] = v.astype(BF16)
    kk = k * kk_ref[...]
    kap = kk / jnp.maximum(jnp.sqrt(_head_sum(kk * kk, ones)), 1e-12)
    sh_ref[2] = kap.astype(BF16)
    sh_ref[3] = jnp.dot(_sigmoid(xg).astype(BF16), gup_ref[...], preferred_element_type=F32).astype(BF16)

    tw = jnp.tanh(xw).astype(BF16)
    xab = xa.astype(BF16)
    kdsum = None
    for d in range(2):
        x = w0_ref[d] + jnp.dot(tw, wup_ref[d], preferred_element_type=F32)
        lw_ref[d] = (-math.exp(-0.5)) * _sigmoid(x)
        eta = _sigmoid(a0_ref[d] + jnp.dot(xab, aup_ref[d], preferred_element_type=F32))
        pd_ref[d, 0] = (kap * eta).astype(BF16)
        kd = k * (1.0 + (eta - 1.0) * ka_ref[...])
        pd_ref[d, 1] = kd.astype(BF16)
        kdsum = kd if kdsum is None else kdsum + kd
    sh_ref[4] = (_head_sum(r * kdsum * rk_ref[...], ones) * v).astype(BF16)


def _rwkv_prep(z_c, seq_len, p):
    m, wz = z_c.shape
    wc = p["rwkv_k_k"].shape[1]
    rw, ra = p["rwkv_w_up"].shape[1], p["rwkv_a_up"].shape[1]
    tm = _pick(seq_len, (128, 64, 32, 16))
    full = lambda a: pl.BlockSpec(a.shape, lambda i: (0,) * a.ndim)
    consts = [p["ts_mu_prev"], p["ts_mu_next"], p["rwkv_w0"], p["rwkv_w_up"], p["rwkv_a0"], p["rwkv_a_up"],
              p["rwkv_g_up"], p["rwkv_k_k"], p["rwkv_k_a"], p["rwkv_r_k"]]
    return pl.pallas_call(
        functools.partial(_prep_body, tm=tm, seq_len=seq_len, wc=wc, rw=rw, ra=ra),
        grid=(m // tm,),
        in_specs=_halo_specs(tm, wz, 0, m) + [full(a) for a in consts],
        out_specs=[pl.BlockSpec((5, tm, wc), lambda i: (0, i, 0)),
                   pl.BlockSpec((2, tm, wc), lambda i: (0, i, 0)),
                   pl.BlockSpec((2, 2, tm, wc), lambda i: (0, 0, i, 0))],
        out_shape=[jax.ShapeDtypeStruct((5, m, wc), BF16),
                   jax.ShapeDtypeStruct((2, m, wc), F32),
                   jax.ShapeDtypeStruct((2, 2, m, wc), BF16)],
        scratch_shapes=[pltpu.VMEM((tm + 2 * SUBLANES, wz), F32)],
        compiler_params=_params("parallel"),
        name="rwkv_prep",
    )(z_c, z_c, z_c, *consts)


_NN = (((1,), (0,)), ((), ()))
_NT = (((1,), (1,)), ((), ()))
_TN = (((0,), (0,)), ((), ()))
WKV_GROUP = 16


def _split(x, pieces):
    hi = x.astype(BF16)
    if pieces == 1:
        return (hi,)
    return (hi, (x - hi.astype(F32)).astype(BF16))


def _mm(xs, ys, dims=_NN):
    ax_x, ax_y = dims[0][0][0], dims[0][1][0]
    if len(xs) == 1:
        x, y = xs[0], ys[0]
    else:
        x = jnp.concatenate([xs[0], xs[1], xs[0]], axis=ax_x)
        y = jnp.concatenate([ys[0], ys[0], ys[1]], axis=ax_y)
    return lax.dot_general(x, y, dims, preferred_element_type=F32)


def _wkv_body(r_ref, v_ref, kap_ref, lw_ref, ke_ref, kd_ref, s0_ref, o_ref, so_ref,
              s_ref, a_sc, b_sc, k_sc, r_sc, b2_sc, k2_sc, g_sc, *, nc, npairs, group, inv_pieces):
    c = CHUNK
    d = pl.program_id(1)
    ci = pl.program_id(2)

    @pl.when(ci == 0)
    def _():
        zero = jnp.zeros((HEAD, HEAD), F32)
        for p in range(npairs):
            top = jnp.concatenate([s0_ref[0, 0, 2 * p], zero], axis=1)
            bot = jnp.concatenate([zero, s0_ref[0, 0, 2 * p + 1]], axis=1)
            s_ref[p] = jnp.concatenate([top, bot], axis=0)

    row = lax.broadcasted_iota(jnp.int32, (c, 2 * c), 0)
    col = lax.broadcasted_iota(jnp.int32, (c, 2 * c), 1) % c
    fwd = d == 0
    ahead = jnp.where(fwd, row - col, col - row)
    strict2 = ahead > 0
    incl2 = ahead >= 0
    lw = lw_ref[0]
    l1 = lw.astype(BF16)
    rem = lw - l1.astype(F32)
    l2 = rem.astype(BF16)
    l3 = (rem - l2.astype(F32)).astype(BF16)
    tri = incl2[:, :c].astype(F32).astype(BF16)
    cum = jnp.dot(jnp.concatenate([tri, tri, tri], axis=1), jnp.concatenate([l1, l2, l3], axis=0),
                  preferred_element_type=F32)
    tot = jnp.where(fwd, cum[c - 1:c, :], cum[0:1, :])
    ke = ke_ref[0, 0].astype(F32)
    kd = kd_ref[0, 0].astype(F32)
    gtot = jnp.exp(tot)
    a_sc[...] = -kap_ref[0].astype(F32) * jnp.exp(cum - lw)
    em = jnp.exp(-cum)
    b_sc[...] = ke * em
    k_sc[...] = kd * em
    r_sc[...] = r_ref[0].astype(F32) * jnp.exp(cum)
    e2 = gtot * em
    b2_sc[...] = ke * e2
    k2_sc[...] = kd * e2
    g_sc[...] = jnp.broadcast_to(gtot, g_sc.shape)

    lane = lax.broadcasted_iota(jnp.int32, (1, LANES), 1)
    m0 = lane < HEAD
    rbd = lax.broadcasted_iota(jnp.int32, (LANES, LANES), 0) // HEAD
    cbd = lax.broadcasted_iota(jnp.int32, (LANES, LANES), 1) // HEAD
    bdmask = rbd == cbd
    m0w = jnp.concatenate([m0, m0], axis=1)
    levels = c.bit_length() - 1

    def stack(x):
        m = m0 if x.shape[1] == LANES else m0w
        return jnp.concatenate([jnp.where(m, x, 0.0), jnp.where(m, 0.0, x)], axis=0)

    def bdot(x, y, dims=_NN):
        return lax.dot_general(x.astype(BF16), y.astype(BF16), dims, preferred_element_type=F32)

    def process(sls, sidx):
        n = len(sls)
        mbd, xbd, lr, vst, sbd = [], [], [], [], []
        for q in range(n):
            sl = sls[q]
            lm = bdot(jnp.concatenate([a_sc[:, sl], r_sc[:, sl]], axis=0),
                      jnp.concatenate([stack(b_sc[:, sl]), stack(k_sc[:, sl])], axis=0), _NT)
            lak = jnp.where(strict2, lm[:c, 2 * c:], 0.0)
            lr.append(jnp.concatenate([jnp.where(incl2, lm[c:, :2 * c], 0.0),
                                       jnp.where(incl2, lm[c:, 2 * c:], 0.0)], axis=1).astype(BF16))
            vst.append(stack(v_ref[0, :, sl].astype(F32)).astype(BF16))
            sbd.append(s_ref[sidx[q]].astype(BF16))
            xbd.append(bdot(a_sc[:, sl], sbd[q], _NT) + bdot(lak, vst[q]))
            mbd.append(jnp.where(strict2, lm[:c, :2 * c], 0.0))
        for lvl in range(levels):
            for q in range(n):
                lhs = _split(mbd[q], inv_pieces)
                if lvl + 1 < levels:
                    y = _mm(lhs, _split(stack(jnp.concatenate([xbd[q], mbd[q]], axis=1)), inv_pieces))
                    xbd[q] = xbd[q] + y[:, :LANES]
                    mbd[q] = y[:, LANES:]
                else:
                    xbd[q] = xbd[q] + _mm(lhs, _split(stack(xbd[q]), inv_pieces))
        for q in range(n):
            sl = sls[q]
            u = xbd[q]
            o_ref[0, :, sl] = (bdot(r_sc[:, sl], sbd[q], _NT)
                               + bdot(lr[q], jnp.concatenate([stack(u).astype(BF16), vst[q]], axis=0))).astype(BF16)
            upd = bdot(jnp.concatenate([u, v_ref[0, :, sl].astype(F32)], axis=0),
                       jnp.concatenate([b2_sc[:, sl], k2_sc[:, sl]], axis=0), _TN)
            s_ref[sidx[q]] = s_ref[sidx[q]] * g_sc[0:1, sl] + jnp.where(bdmask, upd, 0.0)

    if group >= npairs:
        process([slice(p * LANES, (p + 1) * LANES) for p in range(npairs)], list(range(npairs)))
    else:
        def body(g, carry):
            idx = [g * group + q for q in range(group)]
            process([pl.ds(pl.multiple_of(i * LANES, LANES), LANES) for i in idx], idx)
            return carry
        lax.fori_loop(0, npairs // group, body, 0)

    @pl.when(ci == nc - 1)
    def _():
        for p in range(npairs):
            sbd = s_ref[p]
            so_ref[0, 0, 2 * p] = sbd[:HEAD, :HEAD]
            so_ref[0, 0, 2 * p + 1] = sbd[HEAD:, HEAD:]


def _wkv(shared, logw, perdir, s0, bsz, seq_len, state_is_output):
    m, wc = shared.shape[1], shared.shape[2]
    nheads = wc // HEAD
    npairs = nheads // 2
    c = CHUNK
    nc = seq_len // c

    def chunk(b, d, i):
        return b * nc + jnp.where(d == 0, i, nc - 1 - i)

    sh = lambda j: pl.BlockSpec((1, c, wc), lambda b, d, i: (j, chunk(b, d, i), 0))
    pd = lambda j: pl.BlockSpec((1, 1, c, wc), lambda b, d, i: (d, j, chunk(b, d, i), 0))
    per_dir = pl.BlockSpec((1, c, wc), lambda b, d, i: (d, chunk(b, d, i), 0))
    st = pl.BlockSpec((1, 1, nheads, HEAD, HEAD), lambda b, d, i: (b, d, 0, 0, 0))
    cw = pltpu.VMEM((c, wc), F32)
    return pl.pallas_call(
        functools.partial(_wkv_body, nc=nc, npairs=npairs,
                          group=WKV_GROUP if npairs % WKV_GROUP == 0 else npairs,
                          inv_pieces=2 if state_is_output else 1),
        grid=(bsz, 2, nc),
        in_specs=[sh(0), sh(1), sh(2), per_dir, pd(0), pd(1), st],
        out_specs=[per_dir, st],
        out_shape=[jax.ShapeDtypeStruct((2, m, wc), BF16),
                   jax.ShapeDtypeStruct((bsz, 2, nheads, HEAD, HEAD), F32)],
        scratch_shapes=[pltpu.VMEM((npairs, LANES, LANES), F32), cw, cw, cw, cw, cw, cw,
                        pltpu.VMEM((SUBLANES, wc), F32)],
        compiler_params=_params("parallel", "parallel", "arbitrary"),
        name="wkv_scan",
    )(shared, shared, shared, logw, perdir, perdir, s0)


def _post_body(o_ref, g_ref, bonus_ref, hf_ref, hb_ref, zg_ref, cb_ref, lng_ref, lnb_ref, xg_ref, xb_ref,
               y_ref, *, wa, wb):
    ones = _head_ones()
    y_ref[:, 0:wa] = ((hf_ref[...] + hb_ref[...]) * _gelu_tanh(zg_ref[...])).astype(BF16)

    u = cb_ref[...]
    mu = jnp.mean(u, axis=-1, keepdims=True)
    uc = u - mu
    var = jnp.mean(uc * uc, axis=-1, keepdims=True)
    un = uc * lax.rsqrt(var + LN_EPS) * lng_ref[...] + lnb_ref[...]
    y_ref[:, wa:wa + wb] = (un * _sigmoid(un)).astype(BF16)

    o = o_ref[0].astype(F32) + o_ref[1].astype(F32)
    inv = 1.0 / HEAD
    omu = _head_sum(o, ones) * inv
    oc = o - omu
    ovar = _head_sum(oc * oc, ones) * inv
    on = oc * lax.rsqrt(ovar + GN_EPS) * xg_ref[...] + xb_ref[...]
    y_ref[:, wa + wb:] = ((on + bonus_ref[0]) * g_ref[0]).astype(BF16)


def _post(o, shared, hf, hb, z_ab, cb, seq_len, p):
    m, wc = o.shape[1], o.shape[2]
    wa, wb = hf.shape[1], cb.shape[1]
    dmix = wa + wb + wc
    tm = _pick(seq_len, (256, 128, 64, 32, 16))
    vec = lambda n: pl.BlockSpec((1, n), lambda i: (0, 0))
    return pl.pallas_call(
        functools.partial(_post_body, wa=wa, wb=wb),
        grid=(m // tm,),
        in_specs=[pl.BlockSpec((2, tm, wc), lambda i: (0, i, 0)),
                  pl.BlockSpec((1, tm, wc), lambda i: (3, i, 0)),
                  pl.BlockSpec((1, tm, wc), lambda i: (4, i, 0)),
                  pl.BlockSpec((tm, wa), lambda i: (i, 0)),
                  pl.BlockSpec((tm, wa), lambda i: (i, 0)),
                  pl.BlockSpec((tm, wa), lambda i: (i, 1)),
                  pl.BlockSpec((tm, wb), lambda i: (i, 0)),
                  vec(wb), vec(wb), vec(wc), vec(wc)],
        out_specs=pl.BlockSpec((tm, dmix), lambda i: (i, 0)),
        out_shape=jax.ShapeDtypeStruct((m, dmix), BF16),
        compiler_params=_params("parallel"),
        name="mix_post",
    )(o, shared, shared, hf, hb, z_ab, cb, p["conv_ln_g"], p["conv_ln_b"], p["rwkv_lnx_g"], p["rwkv_lnx_b"])


def _layer(x, h, bsz, seq_len, mods, nxt, p, h0_lru, s0_wkv, on_grid, keep_state, alpha):
    wa = p["conv_a_w"].shape[1]
    l = p["layer"]
    z_ab = _matmul(h, p["w_in_ab"], l, F32, name="w_in_ab")
    z_c = _matmul(h, p["w_in_c"], l, F32, name="w_in_c")
    ab = _lru_gates(z_ab, seq_len, p)
    hf, hb, st_lru = _lru_scan(ab, h0_lru, bsz, seq_len)
    cb = _convb(z_ab, bsz, seq_len, 2 * wa, p, on_grid)
    shared, logw, perdir = _rwkv_prep(z_c, seq_len, p)
    o, st_wkv = _wkv(shared, logw, perdir, s0_wkv, bsz, seq_len, state_is_output=keep_state)
    ycat = _post(o, shared, hf, hb, z_ab, cb, seq_len, p)
    pre = _matmul(ycat, p["w_out"], l, F32, res=(x, mods["g1"], seq_len, alpha), name="w_out")
    x, h2 = _ln_mod(pre, seq_len, p["ln1_g"], p["ln1_b"], mod=(mods["sc2"], mods["sh2"]))
    hid = _matmul(h2, p["w_up"], l, BF16, relu2=True, name="ffn_up")
    pre = _matmul(hid, p["w_down"], l, F32, res=(x, mods["g2"], seq_len, alpha), name="ffn_down")
    x, hn = _ln_mod(pre, seq_len, p["ln2_g"], p["ln2_b"], mod=nxt)
    return x, hn, st_lru, st_wkv


def _block_diag_pairs(w):
    two, nb, hd, _ = w.shape
    w = w.reshape(two, nb // 2, 2, hd, hd)
    z = jnp.zeros_like(w[:, :, 0])
    top = jnp.concatenate([w[:, :, 0], z], axis=-1)
    bot = jnp.concatenate([z, w[:, :, 1]], axis=-1)
    return jnp.concatenate([top, bot], axis=-2)


def _pad_cols(a, segs):
    out, c = [], 0
    for w, wp in segs:
        piece = a[..., c:c + w]
        if wp > w:
            piece = jnp.pad(piece, [(0, 0)] * (a.ndim - 1) + [(0, wp - w)])
        out.append(piece)
        c += w
    return jnp.concatenate(out, axis=-1)


def kernel(x_prompt, x_sample, state_lru, state_wkv, c, c_ctx, ln_in_g, ln_in_b, w_mod, b_mod, w_in, w_out, conv_a_w, conv_a_b, lru_ga_w, lru_ga_b, lru_gx_w, lru_gx_b, lru_lambda, conv_b_w, conv_b_b, conv_ln_g, conv_ln_b, ts_mu_prev, ts_mu_next, rwkv_w0, rwkv_w_up, rwkv_a0, rwkv_a_up, rwkv_g_up, rwkv_k_k, rwkv_k_a, rwkv_r_k, rwkv_lnx_g, rwkv_lnx_b, ln1_g, ln1_b, w_up, w_down, ln2_g, ln2_b):
    bp, tp, d = x_prompt.shape
    bl, tl, _ = x_sample.shape
    depth = w_in.shape[0]
    wa = conv_a_w.shape[-1]
    wb = conv_b_w.shape[-1]
    wc = rwkv_k_k.shape[-1]
    rw, ra, rg = rwkv_w_up.shape[2], rwkv_a_up.shape[2], rwkv_g_up.shape[1]
    rwp, rap = _round_up(rw, LANES), _round_up(ra, LANES)
    nheads = wc // HEAD
    alpha = (2 * depth) ** 0.25
    assert CHUNK == HEAD and tp % CHUNK == 0 and tl % CHUNK == 0 and tl % GRID_W == 0

    rows = _round_up(1 + bl, 2 * SUBLANES)
    cc = jnp.zeros((rows, d), F32).at[0].set(c_ctx).at[1:1 + bl].set(c)
    mod = _modulation(cc, w_mod, b_mod).reshape(depth, rows, 6, 1, d)

    def mods_of(l, lo, hi):
        names = ("sh1", "sc1", "g1", "sh2", "sc2", "g2")
        return {n: mod[l, lo:hi, j] for j, n in enumerate(names)}

    c_segs = [(3 * wc, 3 * wc), (rw, rwp), (ra, rap), (rg, rg)]
    nab = 2 * wa + 2 * wb
    big = {"w_in_ab": w_in[:, :, :nab].astype(BF16), "w_in_c": _pad_cols(w_in[:, :, nab:], c_segs).astype(BF16),
           "w_out": w_out.astype(BF16), "w_up": w_up.astype(BF16), "w_down": w_down.astype(BF16)}
    layers = []
    for l in range(depth):
        layers.append({
            "layer": l, **big,
            "conv_a_w": conv_a_w[l], "conv_a_b": conv_a_b[l].reshape(1, wa),
            "lru_ga_w": _block_diag_pairs(lru_ga_w[l]).astype(BF16), "lru_ga_b": lru_ga_b[l].reshape(2, 1, wa),
            "lru_gx_w": _block_diag_pairs(lru_gx_w[l]).astype(BF16), "lru_gx_b": lru_gx_b[l].reshape(2, 1, wa),
            "lru_lambda": lru_lambda[l].reshape(2, 1, wa),
            "conv_b_w": conv_b_w[l], "conv_b_b": conv_b_b[l].reshape(1, wb),
            "conv_ln_g": conv_ln_g[l].reshape(1, wb), "conv_ln_b": conv_ln_b[l].reshape(1, wb),
            "ts_mu_prev": _pad_cols(ts_mu_prev[l].reshape(1, -1), c_segs),
            "ts_mu_next": _pad_cols(ts_mu_next[l].reshape(1, -1), c_segs),
            "rwkv_w0": rwkv_w0[l].reshape(2, 1, wc),
            "rwkv_w_up": jnp.pad(rwkv_w_up[l], ((0, 0), (0, rwp - rw), (0, 0))).astype(BF16),
            "rwkv_a0": rwkv_a0[l].reshape(2, 1, wc),
            "rwkv_a_up": jnp.pad(rwkv_a_up[l], ((0, 0), (0, rap - ra), (0, 0))).astype(BF16),
            "rwkv_g_up": rwkv_g_up[l].astype(BF16),
            "rwkv_k_k": rwkv_k_k[l].reshape(1, wc), "rwkv_k_a": rwkv_k_a[l].reshape(1, wc),
            "rwkv_r_k": rwkv_r_k[l].reshape(1, wc),
            "rwkv_lnx_g": rwkv_lnx_g[l].reshape(1, wc), "rwkv_lnx_b": rwkv_lnx_b[l].reshape(1, wc),
            "ln1_g": ln1_g[l], "ln1_b": ln1_b[l], "ln2_g": ln2_g[l], "ln2_b": ln2_b[l],
        })

    m0c, m0l = mods_of(0, 0, 1), mods_of(0, 1, 1 + bl)
    xp, hp = _ln_mod(x_prompt.reshape(bp * tp, d), tp, ln_in_g, ln_in_b, mod=(m0c["sc1"], m0c["sh1"]))
    xs, hs = _ln_mod(x_sample.reshape(bl * tl, d), tl, ln_in_g, ln_in_b, mod=(m0l["sc1"], m0l["sh1"]))
    h0_ctx = jnp.zeros((bp, 2, wa), F32)
    s0_ctx = jnp.zeros((bp, 2, nheads, HEAD, HEAD), F32)
    lru_states, wkv_states = [], []
    for l in range(depth):
        mc, ml = mods_of(l, 0, 1), mods_of(l, 1, 1 + bl)
        if l + 1 < depth:
            nc_, nl_ = mods_of(l + 1, 0, 1), mods_of(l + 1, 1, 1 + bl)
            nxt_c, nxt_l = (nc_["sc1"], nc_["sh1"]), (nl_["sc1"], nl_["sh1"])
        else:
            nxt_c = nxt_l = None
        xp, hp, st_lru, st_wkv = _layer(xp, hp, bp, tp, mc, nxt_c, layers[l], h0_ctx, s0_ctx, False, True, alpha)
        lru_states.append(st_lru)
        wkv_states.append(st_wkv)
        xs, hs, _, _ = _layer(xs, hs, bl, tl, ml, nxt_l, layers[l], state_lru[:, l].astype(F32),
                              state_wkv[:, l].astype(F32), True, False, alpha)
    return (xp.reshape(bp, tp, d), xs.reshape(bl, tl, d),
            jnp.stack(lru_states, axis=1), jnp.stack(wkv_states, axis=1))
```

```python
import functools
import math

import jax
import jax.numpy as jnp
from jax import lax
from jax.experimental import pallas as pl
from jax.experimental.pallas import tpu as pltpu

F32 = jnp.float32
BF16 = jnp.bfloat16
HI = lax.Precision.HIGHEST

LANES = 128
SUBLANES = 8
VMEM_LIMIT = 60 * 1024 * 1024
GRID_W = 64
HEAD = 64
CHUNK = 64
C_LRU = 8.0
LN_EPS = 1e-5
GN_EPS = 64e-5


def _pick(n, cands):
    for c in cands:
        if c <= n and n % c == 0:
            return c
    return n


def _round_up(n, m):
    return (n + m - 1) // m * m


def _params(*sem):
    return pltpu.CompilerParams(dimension_semantics=sem, vmem_limit_bytes=VMEM_LIMIT)


def _sigmoid(x):
    return 0.5 * jnp.tanh(0.5 * x) + 0.5


def _softplus(x):
    return jnp.maximum(x, 0.0) + jnp.log(1.0 + jnp.exp(-jnp.abs(x)))


def _gelu_tanh(x):
    return 0.5 * x * (1.0 + jnp.tanh(math.sqrt(2.0 / math.pi) * (x + 0.044715 * (x * x * x))))


def _head_ones():
    r = lax.broadcasted_iota(jnp.int32, (LANES, LANES), 0) // HEAD
    c = lax.broadcasted_iota(jnp.int32, (LANES, LANES), 1) // HEAD
    return (r == c).astype(F32)


def _head_sum(x, ones):
    parts = [jnp.dot(x[:, j * LANES:(j + 1) * LANES], ones, precision=HI, preferred_element_type=F32)
             for j in range(x.shape[1] // LANES)]
    return parts[0] if len(parts) == 1 else jnp.concatenate(parts, axis=1)


def _mod_body(c_ref, w_ref, b_ref, o_ref):
    c = c_ref[...]
    a = (c * _sigmoid(c)).astype(BF16)
    o_ref[0] = jnp.dot(a, w_ref[0].astype(BF16), preferred_element_type=F32) + b_ref[0]


def _modulation(cc, w_mod, b_mod):
    depth, d, n = w_mod.shape
    rows = cc.shape[0]
    tn = _pick(n, (512, 256, 128))
    return pl.pallas_call(
        _mod_body,
        grid=(depth, n // tn),
        in_specs=[pl.BlockSpec((rows, d), lambda l, j: (0, 0)),
                  pl.BlockSpec((1, d, tn), lambda l, j: (l, 0, j)),
                  pl.BlockSpec((1, 1, tn), lambda l, j: (l, 0, j))],
        out_specs=pl.BlockSpec((1, rows, tn), lambda l, j: (l, 0, j)),
        out_shape=jax.ShapeDtypeStruct((depth, rows, n), F32),
        compiler_params=_params("parallel", "parallel"),
        name="modulation",
    )(cc, w_mod, b_mod.reshape(depth, 1, n))


def _mm_body(a_ref, w_ref, *rest, nk, relu2, alpha):
    if alpha is not None:
        x_ref, gate_ref, o_ref, *scratch = rest
    else:
        o_ref, *scratch = rest

    def finish(r):
        if relu2:
            r = jnp.square(jnp.maximum(r, 0.0))
        if alpha is not None:
            r = alpha * x_ref[...] + gate_ref[0] * r
        o_ref[...] = r.astype(o_ref.dtype)

    if nk == 1:
        finish(jnp.dot(a_ref[...], w_ref[...], preferred_element_type=F32))
        return
    acc_ref, = scratch
    k = pl.program_id(2)

    @pl.when(k == 0)
    def _():
        acc_ref[...] = jnp.zeros_like(acc_ref)

    acc_ref[...] += jnp.dot(a_ref[...], w_ref[...], preferred_element_type=F32)

    @pl.when(k == nk - 1)
    def _():
        finish(acc_ref[...])


def _matmul(a, w, layer, out_dtype, relu2=False, res=None, name="matmul"):
    m, kdim = a.shape
    n = w.shape[2]
    per_seq = res is not None and res[1].shape[0] > 1
    seq_len = res[2] if per_seq else m
    tm = _pick(seq_len, (1024, 512, 256, 128, 64, 32, 16))
    tn = _pick(n, (1024, 512, 256, 128))
    tk = _pick(kdim, (4096, 2048, 1024, 512, 256, 128))
    nk = kdim // tk
    args = [a, w]
    specs = [pl.BlockSpec((tm, tk), lambda i, j, k: (i, k)),
             pl.BlockSpec((None, tk, tn), lambda i, j, k: (layer, k, j))]
    alpha = None
    if res is not None:
        x, gate, _, alpha = res
        args += [x, gate]
        specs += [pl.BlockSpec((tm, tn), lambda i, j, k: (i, j)),
                  pl.BlockSpec((1, 1, tn), lambda i, j, k: ((i * tm) // seq_len if per_seq else 0, 0, j))]
    return pl.pallas_call(
        functools.partial(_mm_body, nk=nk, relu2=relu2, alpha=alpha),
        grid=(m // tm, n // tn, nk),
        in_specs=specs,
        out_specs=pl.BlockSpec((tm, tn), lambda i, j, k: (i, j)),
        out_shape=jax.ShapeDtypeStruct((m, n), out_dtype),
        scratch_shapes=[pltpu.VMEM((tm, tn), F32)] if nk > 1 else [],
        compiler_params=_params("parallel", "parallel", "arbitrary"),
        name=name,
    )(*args)


def _ln_body(*refs, has_mod):
    it = iter(refs)
    x_ref, g_ref, b_ref = next(it), next(it), next(it)
    if has_mod:
        sc_ref, sh_ref = next(it), next(it)
    xo_ref = next(it)
    x = x_ref[...]
    mu = jnp.mean(x, axis=-1, keepdims=True)
    xc = x - mu
    var = jnp.mean(xc * xc, axis=-1, keepdims=True)
    xn = xc * lax.rsqrt(var + LN_EPS) * g_ref[...] + b_ref[...]
    xo_ref[...] = xn
    if has_mod:
        h_ref = next(it)
        h_ref[...] = (xn * (1.0 + sc_ref[0]) + sh_ref[0]).astype(BF16)


def _ln_mod(x, seq_len, g, b, mod=None):
    m, d = x.shape
    tm = _pick(seq_len, (256, 128, 64, 32, 16))

    def per_seq(arr):
        if arr.shape[0] == 1:
            return pl.BlockSpec((1, 1, d), lambda i: (0, 0, 0))
        return pl.BlockSpec((1, 1, d), lambda i: ((i * tm) // seq_len, 0, 0))

    row = pl.BlockSpec((tm, d), lambda i: (i, 0))
    vec = pl.BlockSpec((1, d), lambda i: (0, 0))
    args, specs = [x, g.reshape(1, d), b.reshape(1, d)], [row, vec, vec]
    if mod is not None:
        args += [mod[0], mod[1]]
        specs += [per_seq(mod[0]), per_seq(mod[1])]
    out_shape = [jax.ShapeDtypeStruct((m, d), F32)]
    out_specs = [row]
    if mod is not None:
        out_shape.append(jax.ShapeDtypeStruct((m, d), BF16))
        out_specs.append(row)
    out = pl.pallas_call(
        functools.partial(_ln_body, has_mod=mod is not None),
        grid=(m // tm,),
        in_specs=specs,
        out_specs=out_specs,
        out_shape=out_shape,
        compiler_params=_params("parallel"),
        name="ln_mod",
    )(*args)
    return out if mod is not None else (out[0], None)


def _halo_specs(tm, width, col, m):
    nb8 = m // SUBLANES
    r8 = tm // SUBLANES
    return [pl.BlockSpec((tm, width), lambda i: (i, col)),
            pl.BlockSpec((SUBLANES, width), lambda i: (jnp.maximum(i * r8 - 1, 0), col)),
            pl.BlockSpec((SUBLANES, width), lambda i: (jnp.minimum((i + 1) * r8, nb8 - 1), col))]


def _fill_ext(ext_ref, z_ref, zp_ref, zn_ref, tm, seq_len):
    i = pl.program_id(0)
    first = (i * tm) % seq_len == 0
    last = ((i + 1) * tm) % seq_len == 0
    ext_ref[0:SUBLANES] = jnp.where(first, 0.0, zp_ref[...])
    ext_ref[SUBLANES:SUBLANES + tm] = z_ref[...]
    ext_ref[SUBLANES + tm:2 * SUBLANES + tm] = jnp.where(last, 0.0, zn_ref[...])


def _lru_gates_body(zx_ref, zp_ref, zn_ref, cw_ref, cb_ref, gaw_ref, gab_ref, gxw_ref, gxb_ref, lam_ref,
                    o_ref, ext_ref, *, tm, seq_len, nj, taps):
    _fill_ext(ext_ref, zx_ref, zp_ref, zn_ref, tm, seq_len)
    u = cb_ref[...] + cw_ref[0:1, :] * ext_ref[SUBLANES - 1:SUBLANES - 1 + tm, :]
    for k in range(1, taps):
        u = u + cw_ref[k:k + 1, :] * ext_ref[SUBLANES - 1 + k:SUBLANES - 1 + k + tm, :]
    for j in range(nj):
        sl = slice(j * LANES, (j + 1) * LANES)
        uj = u[:, sl]
        ub = uj.astype(BF16)
        for d in range(2):
            r = _sigmoid(jnp.dot(ub, gaw_ref[d, j], preferred_element_type=F32) + gab_ref[d][:, sl])
            ig = _sigmoid(jnp.dot(ub, gxw_ref[d, j], preferred_element_type=F32) + gxb_ref[d][:, sl])
            log_a = (-C_LRU) * r * _softplus(-lam_ref[d][:, sl])
            a = jnp.exp(log_a)
            bb = jnp.sqrt(-jnp.tanh(log_a) * (a * a + 1.0)) * (ig * uj)
            o_ref[2 * d, :, sl] = a
            o_ref[2 * d + 1, :, sl] = bb


def _lru_gates(z_ab, seq_len, p):
    m = z_ab.shape[0]
    wa = p["conv_a_w"].shape[1]
    taps = p["conv_a_w"].shape[0]
    nj = wa // LANES
    tm = _pick(seq_len, (256, 128, 64, 32, 16))
    full = lambda a: pl.BlockSpec(a.shape, lambda i: (0,) * a.ndim)
    consts = [p["conv_a_w"], p["conv_a_b"], p["lru_ga_w"], p["lru_ga_b"], p["lru_gx_w"], p["lru_gx_b"],
              p["lru_lambda"]]
    return pl.pallas_call(
        functools.partial(_lru_gates_body, tm=tm, seq_len=seq_len, nj=nj, taps=taps),
        grid=(m // tm,),
        in_specs=_halo_specs(tm, wa, 0, m) + [full(a) for a in consts],
        out_specs=pl.BlockSpec((4, tm, wa), lambda i: (0, i, 0)),
        out_shape=jax.ShapeDtypeStruct((4, m, wa), F32),
        scratch_shapes=[pltpu.VMEM((tm + 2 * SUBLANES, wa), F32)],
        compiler_params=_params("parallel"),
        name="lru_gates",
    )(z_ab, z_ab, z_ab, *consts)


def _lru_scan_body(af_ref, bf_ref, ab_ref, bb_ref, h0_ref, hf_ref, hb_ref, st_ref, carry_ref, *, tt, nt):
    i = pl.program_id(1)

    @pl.when(i == 0)
    def _():
        carry_ref[...] = h0_ref[0]

    def step(t, c):
        hf, hb = c
        hf = af_ref[0, 0, t] * hf + bf_ref[0, 0, t]
        hf_ref[0, t] = hf
        tb = tt - 1 - t
        hb = ab_ref[0, 0, tb] * hb + bb_ref[0, 0, tb]
        hb_ref[0, tb] = hb
        return hf, hb

    hf, hb = lax.fori_loop(0, tt, step, (carry_ref[0], carry_ref[1]), unroll=8)
    carry_ref[0] = hf
    carry_ref[1] = hb

    @pl.when(i == nt - 1)
    def _():
        st_ref[0, 0] = hf
        st_ref[0, 1] = hb


def _lru_scan(ab, h0, bsz, seq_len):
    wa = ab.shape[-1]
    s8 = wa // LANES
    ab5 = ab.reshape(4, bsz, seq_len, s8, LANES)
    h0 = h0.reshape(bsz, 2, s8, LANES)
    tt = _pick(seq_len, (256, 128, 64, 32, 16, 8))
    nt = seq_len // tt
    blk = (1, 1, tt, s8, LANES)
    oblk = (1, tt, s8, LANES)
    hf, hb, st = pl.pallas_call(
        functools.partial(_lru_scan_body, tt=tt, nt=nt),
        grid=(bsz, nt),
        in_specs=[pl.BlockSpec(blk, lambda b, i: (0, b, i, 0, 0)),
                  pl.BlockSpec(blk, lambda b, i: (1, b, i, 0, 0)),
                  pl.BlockSpec(blk, lambda b, i: (2, b, nt - 1 - i, 0, 0)),
                  pl.BlockSpec(blk, lambda b, i: (3, b, nt - 1 - i, 0, 0)),
                  pl.BlockSpec((1, 2, s8, LANES), lambda b, i: (b, 0, 0, 0))],
        out_specs=[pl.BlockSpec(oblk, lambda b, i: (b, i, 0, 0)),
                   pl.BlockSpec(oblk, lambda b, i: (b, nt - 1 - i, 0, 0)),
                   pl.BlockSpec((1, 2, s8, LANES), lambda b, i: (b, 0, 0, 0))],
        out_shape=[jax.ShapeDtypeStruct((bsz, seq_len, s8, LANES), F32),
                   jax.ShapeDtypeStruct((bsz, seq_len, s8, LANES), F32),
                   jax.ShapeDtypeStruct((bsz, 2, s8, LANES), F32)],
        scratch_shapes=[pltpu.VMEM((2, s8, LANES), F32)],
        compiler_params=_params("parallel", "arbitrary"),
        name="lru_scan",
    )(ab5, ab5, ab5, ab5, h0)
    m = bsz * seq_len
    return hf.reshape(m, wa), hb.reshape(m, wa), st.reshape(bsz, 2, wa)


def _convb_body(zv_ref, zg_ref, w_ref, b_ref, o_ref, pad_ref, *, nseq, seq_len, pad, stride, taps, rc):
    zeros = jnp.zeros((pad, LANES), F32)
    half = taps // 2
    nchunks = seq_len // rc
    for s in range(nseq):
        pad_ref[s, 0:pad] = zeros
        pad_ref[s, pad + seq_len:2 * pad + seq_len] = zeros
        pad_ref[s, pad:pad + seq_len] = zv_ref[s] * _sigmoid(zg_ref[s])

        def rows(r0, s=s):
            acc = jnp.broadcast_to(b_ref[...], (rc, LANES))
            for k in range(taps):
                acc = acc + w_ref[k:k + 1, :] * pad_ref[s, pl.ds(r0 + pad + (k - half) * stride, rc), :]
            o_ref[s, pl.ds(r0, rc), :] = acc

        if stride == 1:
            for c in range(nchunks):
                rows(c * rc)
        else:
            def chunk(c, carry, rows=rows):
                rows(pl.multiple_of(c * rc, rc))
                return carry
            lax.fori_loop(0, nchunks, chunk, 0)


def _convb(z_ab, bsz, seq_len, col0, p, on_grid):
    wb = p["conv_b_w"].shape[1]
    taps = p["conv_b_w"].shape[0]
    nj = wb // LANES
    stride = GRID_W if on_grid else 1
    pad = _round_up((taps // 2) * stride, SUBLANES * 2)
    rc = _pick(seq_len, (64, 32, 16, 8)) if stride == 1 else _pick(seq_len, (256, 128, 64))
    z3 = z_ab.reshape(bsz, seq_len, z_ab.shape[1])
    cv, cg = col0 // LANES, (col0 + wb) // LANES
    nseq = 1 if on_grid else _pick(bsz, (4, 2))
    out = pl.pallas_call(
        functools.partial(_convb_body, nseq=nseq, seq_len=seq_len, pad=pad, stride=stride, taps=taps, rc=rc),
        grid=(bsz // nseq, nj),
        in_specs=[pl.BlockSpec((nseq, seq_len, LANES), lambda b, j: (b, 0, cv + j)),
                  pl.BlockSpec((nseq, seq_len, LANES), lambda b, j: (b, 0, cg + j)),
                  pl.BlockSpec((taps, LANES), lambda b, j: (0, j)),
                  pl.BlockSpec((1, LANES), lambda b, j: (0, j))],
        out_specs=pl.BlockSpec((nseq, seq_len, LANES), lambda b, j: (b, 0, j)),
        out_shape=jax.ShapeDtypeStruct((bsz, seq_len, wb), F32),
        scratch_shapes=[pltpu.VMEM((nseq, seq_len + 2 * pad, LANES), F32)],
        compiler_params=_params("parallel", "parallel"),
        name="convb_grid" if on_grid else "convb_seq",
    )(z3, z3, p["conv_b_w"], p["conv_b_b"])
    return out.reshape(bsz * seq_len, wb)


def _prep_body(z_ref, zp_ref, zn_ref, mup_ref, mun_ref, w0_ref, wup_ref, a0_ref, aup_ref, gup_ref,
               kk_ref, ka_ref, rk_ref, sh_ref, lw_ref, pd_ref, ext_ref, *, tm, seq_len, wc, rw, ra):
    _fill_ext(ext_ref, z_ref, zp_ref, zn_ref, tm, seq_len)
    ones = _head_ones()

    def shifted(c0, c1):
        z = ext_ref[SUBLANES:SUBLANES + tm, c0:c1]
        zp = ext_ref[SUBLANES - 1:SUBLANES - 1 + tm, c0:c1]
        zn = ext_ref[SUBLANES + 1:SUBLANES + 1 + tm, c0:c1]
        return z + mup_ref[:, c0:c1] * (zp - z) + mun_ref[:, c0:c1] * (zn - z)

    r = shifted(0, wc)
    k = shifted(wc, 2 * wc)
    v = shifted(2 * wc, 3 * wc)
    xw = shifted(3 * wc, 3 * wc + rw)
    xa = shifted(3 * wc + rw, 3 * wc + rw + ra)
    xg = shifted(3 * wc + rw + ra, z_ref.shape[1])

    sh_ref[0] = r.astype(BF16)
    sh_ref[1] = v.astype(BF16)
    kk = k * kk_ref[...]
    kap = kk / jnp.maximum(jnp.sqrt(_head_sum(kk * kk, ones)), 1e-12)
    sh_ref[2] = kap.astype(BF16)
    sh_ref[3] = jnp.dot(_sigmoid(xg).astype(BF16), gup_ref[...], preferred_element_type=F32).astype(BF16)

    tw = jnp.tanh(xw).astype(BF16)
    xab = xa.astype(BF16)
    kdsum = None
    for d in range(2):
        x = w0_ref[d] + jnp.dot(tw, wup_ref[d], preferred_element_type=F32)
        lw_ref[d] = (-math.exp(-0.5)) * _sigmoid(x)
        eta = _sigmoid(a0_ref[d] + jnp.dot(xab, aup_ref[d], preferred_element_type=F32))
        pd_ref[d, 0] = (kap * eta).astype(BF16)
        kd = k * (1.0 + (eta - 1.0) * ka_ref[...])
        pd_ref[d, 1] = kd.astype(BF16)
        kdsum = kd if kdsum is None else kdsum + kd
    sh_ref[4] = (_head_sum(r * kdsum * rk_ref[...], ones) * v).astype(BF16)


def _rwkv_prep(z_c, seq_len, p):
    m, wz = z_c.shape
    wc = p["rwkv_k_k"].shape[1]
    rw, ra = p["rwkv_w_up"].shape[1], p["rwkv_a_up"].shape[1]
    tm = _pick(seq_len, (128, 64, 32, 16))
    full = lambda a: pl.BlockSpec(a.shape, lambda i: (0,) * a.ndim)
    consts = [p["ts_mu_prev"], p["ts_mu_next"], p["rwkv_w0"], p["rwkv_w_up"], p["rwkv_a0"], p["rwkv_a_up"],
              p["rwkv_g_up"], p["rwkv_k_k"], p["rwkv_k_a"], p["rwkv_r_k"]]
    return pl.pallas_call(
        functools.partial(_prep_body, tm=tm, seq_len=seq_len, wc=wc, rw=rw, ra=ra),
        grid=(m // tm,),
        in_specs=_halo_specs(tm, wz, 0, m) + [full(a) for a in consts],
        out_specs=[pl.BlockSpec((5, tm, wc), lambda i: (0, i, 0)),
                   pl.BlockSpec((2, tm, wc), lambda i: (0, i, 0)),
                   pl.BlockSpec((2, 2, tm, wc), lambda i: (0, 0, i, 0))],
        out_shape=[jax.ShapeDtypeStruct((5, m, wc), BF16),
                   jax.ShapeDtypeStruct((2, m, wc), F32),
                   jax.ShapeDtypeStruct((2, 2, m, wc), BF16)],
        scratch_shapes=[pltpu.VMEM((tm + 2 * SUBLANES, wz), F32)],
        compiler_params=_params("parallel"),
        name="rwkv_prep",
    )(z_c, z_c, z_c, *consts)


_NN = (((1,), (0,)), ((), ()))
_NT = (((1,), (1,)), ((), ()))
_TN = (((0,), (0,)), ((), ()))
WKV_GROUP = 16


def _split(x, pieces):
    hi = x.astype(BF16)
    if pieces == 1:
        return (hi,)
    return (hi, (x - hi.astype(F32)).astype(BF16))


def _mm(xs, ys, dims=_NN):
    ax_x, ax_y = dims[0][0][0], dims[0][1][0]
    if len(xs) == 1:
        x, y = xs[0], ys[0]
    else:
        x = jnp.concatenate([xs[0], xs[1], xs[0]], axis=ax_x)
        y = jnp.concatenate([ys[0], ys[0], ys[1]], axis=ax_y)
    return lax.dot_general(x, y, dims, preferred_element_type=F32)


def _wkv_body(r_ref, v_ref, kap_ref, lw_ref, ke_ref, kd_ref, s0_ref, o_ref, so_ref,
              s_ref, a_sc, b_sc, k_sc, r_sc, b2_sc, k2_sc, g_sc, *, nc, npairs, group, inv_pieces):
    c = CHUNK
    d = pl.program_id(1)
    ci = pl.program_id(2)

    @pl.when(ci == 0)
    def _():
        zero = jnp.zeros((HEAD, HEAD), F32)
        for p in range(npairs):
            top = jnp.concatenate([s0_ref[0, 0, 2 * p], zero], axis=1)
            bot = jnp.concatenate([zero, s0_ref[0, 0, 2 * p + 1]], axis=1)
            s_ref[p] = jnp.concatenate([top, bot], axis=0)

    row = lax.broadcasted_iota(jnp.int32, (c, 2 * c), 0)
    col = lax.broadcasted_iota(jnp.int32, (c, 2 * c), 1) % c
    fwd = d == 0
    ahead = jnp.where(fwd, row - col, col - row)
    strict2 = ahead > 0
    incl2 = ahead >= 0
    lw = lw_ref[0]
    l1 = lw.astype(BF16)
    rem = lw - l1.astype(F32)
    l2 = rem.astype(BF16)
    l3 = (rem - l2.astype(F32)).astype(BF16)
    tri = incl2[:, :c].astype(F32).astype(BF16)
    cum = jnp.dot(jnp.concatenate([tri, tri, tri], axis=1), jnp.concatenate([l1, l2, l3], axis=0),
                  preferred_element_type=F32)
    tot = jnp.where(fwd, cum[c - 1:c, :], cum[0:1, :])
    ke = ke_ref[0, 0].astype(F32)
    kd = kd_ref[0, 0].astype(F32)
    gtot = jnp.exp(tot)
    a_sc[...] = -kap_ref[0].astype(F32) * jnp.exp(cum - lw)
    em = jnp.exp(-cum)
    b_sc[...] = ke * em
    k_sc[...] = kd * em
    r_sc[...] = r_ref[0].astype(F32) * jnp.exp(cum)
    e2 = gtot * em
    b2_sc[...] = ke * e2
    k2_sc[...] = kd * e2
    g_sc[...] = jnp.broadcast_to(gtot, g_sc.shape)

    lane = lax.broadcasted_iota(jnp.int32, (1, LANES), 1)
    m0 = lane < HEAD
    rbd = lax.broadcasted_iota(jnp.int32, (LANES, LANES), 0) // HEAD
    cbd = lax.broadcasted_iota(jnp.int32, (LANES, LANES), 1) // HEAD
    bdmask = rbd == cbd
    m0w = jnp.concatenate([m0, m0], axis=1)
    levels = c.bit_length() - 1

    def stack(x):
        m = m0 if x.shape[1] == LANES else m0w
        return jnp.concatenate([jnp.where(m, x, 0.0), jnp.where(m, 0.0, x)], axis=0)

    def bdot(x, y, dims=_NN):
        return lax.dot_general(x.astype(BF16), y.astype(BF16), dims, preferred_element_type=F32)

    def process(sls, sidx):
        n = len(sls)
        mbd, xbd, lr, vst, sbd = [], [], [], [], []
        for q in range(n):
            sl = sls[q]
            lm = bdot(jnp.concatenate([a_sc[:, sl], r_sc[:, sl]], axis=0),
                      jnp.concatenate([stack(b_sc[:, sl]), stack(k_sc[:, sl])], axis=0), _NT)
            lak = jnp.where(strict2, lm[:c, 2 * c:], 0.0)
            lr.append(jnp.concatenate([jnp.where(incl2, lm[c:, :2 * c], 0.0),
                                       jnp.where(incl2, lm[c:, 2 * c:], 0.0)], axis=1).astype(BF16))
            vst.append(stack(v_ref[0, :, sl].astype(F32)).astype(BF16))
            sbd.append(s_ref[sidx[q]].astype(BF16))
            xbd.append(bdot(a_sc[:, sl], sbd[q], _NT) + bdot(lak, vst[q]))
            mbd.append(jnp.where(strict2, lm[:c, :2 * c], 0.0))
        for lvl in range(levels):
            for q in range(n):
                lhs = _split(mbd[q], inv_pieces)
                if lvl + 1 < levels:
                    y = _mm(lhs, _split(stack(jnp.concatenate([xbd[q], mbd[q]], axis=1)), inv_pieces))
                    xbd[q] = xbd[q] + y[:, :LANES]
                    mbd[q] = y[:, LANES:]
                else:
                    xbd[q] = xbd[q] + _mm(lhs, _split(stack(xbd[q]), inv_pieces))
        for q in range(n):
            sl = sls[q]
            u = xbd[q]
            o_ref[0, :, sl] = (bdot(r_sc[:, sl], sbd[q], _NT)
                               + bdot(lr[q], jnp.concatenate([stack(u).astype(BF16), vst[q]], axis=0))).astype(BF16)
            upd = bdot(jnp.concatenate([u, v_ref[0, :, sl].astype(F32)], axis=0),
                       jnp.concatenate([b2_sc[:, sl], k2_sc[:, sl]], axis=0), _TN)
            s_ref[sidx[q]] = s_ref[sidx[q]] * g_sc[0:1, sl] + jnp.where(bdmask, upd, 0.0)

    if group >= npairs:
        process([slice(p * LANES, (p + 1) * LANES) for p in range(npairs)], list(range(npairs)))
    else:
        def body(g, carry):
            idx = [g * group + q for q in range(group)]
            process([pl.ds(pl.multiple_of(i * LANES, LANES), LANES) for i in idx], idx)
            return carry
        lax.fori_loop(0, npairs // group, body, 0)

    @pl.when(ci == nc - 1)
    def _():
        for p in range(npairs):
            sbd = s_ref[p]
            so_ref[0, 0, 2 * p] = sbd[:HEAD, :HEAD]
            so_ref[0, 0, 2 * p + 1] = sbd[HEAD:, HEAD:]


def _wkv(shared, logw, perdir, s0, bsz, seq_len, state_is_output):
    m, wc = shared.shape[1], shared.shape[2]
    nheads = wc // HEAD
    npairs = nheads // 2
    c = CHUNK
    nc = seq_len // c

    def chunk(b, d, i):
        return b * nc + jnp.where(d == 0, i, nc - 1 - i)

    sh = lambda j: pl.BlockSpec((1, c, wc), lambda b, d, i: (j, chunk(b, d, i), 0))
    pd = lambda j: pl.BlockSpec((1, 1, c, wc), lambda b, d, i: (d, j, chunk(b, d, i), 0))
    per_dir = pl.BlockSpec((1, c, wc), lambda b, d, i: (d, chunk(b, d, i), 0))
    st = pl.BlockSpec((1, 1, nheads, HEAD, HEAD), lambda b, d, i: (b, d, 0, 0, 0))
    cw = pltpu.VMEM((c, wc), F32)
    return pl.pallas_call(
        functools.partial(_wkv_body, nc=nc, npairs=npairs,
                          group=WKV_GROUP if npairs % WKV_GROUP == 0 else npairs,
                          inv_pieces=2 if state_is_output else 1),
        grid=(bsz, 2, nc),
        in_specs=[sh(0), sh(1), sh(2), per_dir, pd(0), pd(1), st],
        out_specs=[per_dir, st],
        out_shape=[jax.ShapeDtypeStruct((2, m, wc), BF16),
                   jax.ShapeDtypeStruct((bsz, 2, nheads, HEAD, HEAD), F32)],
        scratch_shapes=[pltpu.VMEM((npairs, LANES, LANES), F32), cw, cw, cw, cw, cw, cw,
                        pltpu.VMEM((SUBLANES, wc), F32)],
        compiler_params=_params("parallel", "parallel", "arbitrary"),
        name="wkv_scan",
    )(shared, shared, shared, logw, perdir, perdir, s0)


def _post_body(o_ref, g_ref, bonus_ref, hf_ref, hb_ref, zg_ref, cb_ref, lng_ref, lnb_ref, xg_ref, xb_ref,
               y_ref, *, wa, wb):
    ones = _head_ones()
    y_ref[:, 0:wa] = ((hf_ref[...] + hb_ref[...]) * _gelu_tanh(zg_ref[...])).astype(BF16)

    u = cb_ref[...]
    mu = jnp.mean(u, axis=-1, keepdims=True)
    uc = u - mu
    var = jnp.mean(uc * uc, axis=-1, keepdims=True)
    un = uc * lax.rsqrt(var + LN_EPS) * lng_ref[...] + lnb_ref[...]
    y_ref[:, wa:wa + wb] = (un * _sigmoid(un)).astype(BF16)

    o = o_ref[0].astype(F32) + o_ref[1].astype(F32)
    inv = 1.0 / HEAD
    omu = _head_sum(o, ones) * inv
    oc = o - omu
    ovar = _head_sum(oc * oc, ones) * inv
    on = oc * lax.rsqrt(ovar + GN_EPS) * xg_ref[...] + xb_ref[...]
    y_ref[:, wa + wb:] = ((on + bonus_ref[0]) * g_ref[0]).astype(BF16)


def _post(o, shared, hf, hb, z_ab, cb, seq_len, p):
    m, wc = o.shape[1], o.shape[2]
    wa, wb = hf.shape[1], cb.shape[1]
    dmix = wa + wb + wc
    tm = _pick(seq_len, (256, 128, 64, 32, 16))
    vec = lambda n: pl.BlockSpec((1, n), lambda i: (0, 0))
    return pl.pallas_call(
        functools.partial(_post_body, wa=wa, wb=wb),
        grid=(m // tm,),
        in_specs=[pl.BlockSpec((2, tm, wc), lambda i: (0, i, 0)),
                  pl.BlockSpec((1, tm, wc), lambda i: (3, i, 0)),
                  pl.BlockSpec((1, tm, wc), lambda i: (4, i, 0)),
                  pl.BlockSpec((tm, wa), lambda i: (i, 0)),
                  pl.BlockSpec((tm, wa), lambda i: (i, 0)),
                  pl.BlockSpec((tm, wa), lambda i: (i, 1)),
                  pl.BlockSpec((tm, wb), lambda i: (i, 0)),
                  vec(wb), vec(wb), vec(wc), vec(wc)],
        out_specs=pl.BlockSpec((tm, dmix), lambda i: (i, 0)),
        out_shape=jax.ShapeDtypeStruct((m, dmix), BF16),
        compiler_params=_params("parallel"),
        name="mix_post",
    )(o, shared, shared, hf, hb, z_ab, cb, p["conv_ln_g"], p["conv_ln_b"], p["rwkv_lnx_g"], p["rwkv_lnx_b"])


def _layer(x, h, bsz, seq_len, mods, nxt, p, h0_lru, s0_wkv, on_grid, keep_state, alpha):
    wa = p["conv_a_w"].shape[1]
    l = p["layer"]
    z_ab = _matmul(h, p["w_in_ab"], l, F32, name="w_in_ab")
    z_c = _matmul(h, p["w_in_c"], l, F32, name="w_in_c")
    ab = _lru_gates(z_ab, seq_len, p)
    hf, hb, st_lru = _lru_scan(ab, h0_lru, bsz, seq_len)
    cb = _convb(z_ab, bsz, seq_len, 2 * wa, p, on_grid)
    shared, logw, perdir = _rwkv_prep(z_c, seq_len, p)
    o, st_wkv = _wkv(shared, logw, perdir, s0_wkv, bsz, seq_len, state_is_output=keep_state)
    ycat = _post(o, shared, hf, hb, z_ab, cb, seq_len, p)
    pre = _matmul(ycat, p["w_out"], l, F32, res=(x, mods["g1"], seq_len, alpha), name="w_out")
    x, h2 = _ln_mod(pre, seq_len, p["ln1_g"], p["ln1_b"], mod=(mods["sc2"], mods["sh2"]))
    hid = _matmul(h2, p["w_up"], l, BF16, relu2=True, name="ffn_up")
    pre = _matmul(hid, p["w_down"], l, F32, res=(x, mods["g2"], seq_len, alpha), name="ffn_down")
    x, hn = _ln_mod(pre, seq_len, p["ln2_g"], p["ln2_b"], mod=nxt)
    return x, hn, st_lru, st_wkv


def _block_diag_pairs(w):
    two, nb, hd, _ = w.shape
    w = w.reshape(two, nb // 2, 2, hd, hd)
    z = jnp.zeros_like(w[:, :, 0])
    top = jnp.concatenate([w[:, :, 0], z], axis=-1)
    bot = jnp.concatenate([z, w[:, :, 1]], axis=-1)
    return jnp.concatenate([top, bot], axis=-2)


def _pad_cols(a, segs):
    out, c = [], 0
    for w, wp in segs:
        piece = a[..., c:c + w]
        if wp > w:
            piece = jnp.pad(piece, [(0, 0)] * (a.ndim - 1) + [(0, wp - w)])
        out.append(piece)
        c += w
    return jnp.concatenate(out, axis=-1)


def kernel(x_prompt, x_sample, state_lru, state_wkv, c, c_ctx, ln_in_g, ln_in_b, w_mod, b_mod, w_in, w_out, conv_a_w, conv_a_b, lru_ga_w, lru_ga_b, lru_gx_w, lru_gx_b, lru_lambda, conv_b_w, conv_b_b, conv_ln_g, conv_ln_b, ts_mu_prev, ts_mu_next, rwkv_w0, rwkv_w_up, rwkv_a0, rwkv_a_up, rwkv_g_up, rwkv_k_k, rwkv_k_a, rwkv_r_k, rwkv_lnx_g, rwkv_lnx_b, ln1_g, ln1_b, w_up, w_down, ln2_g, ln2_b):
    bp, tp, d = x_prompt.shape
    bl, tl, _ = x_sample.shape
    depth = w_in.shape[0]
    wa = conv_a_w.shape[-1]
    wb = conv_b_w.shape[-1]
    wc = rwkv_k_k.shape[-1]
    rw, ra, rg = rwkv_w_up.shape[2], rwkv_a_up.shape[2], rwkv_g_up.shape[1]
    rwp, rap = _round_up(rw, LANES), _round_up(ra, LANES)
    nheads = wc // HEAD
    alpha = (2 * depth) ** 0.25
    assert CHUNK == HEAD and tp % CHUNK == 0 and tl % CHUNK == 0 and tl % GRID_W == 0

    rows = _round_up(1 + bl, 2 * SUBLANES)
    cc = jnp.zeros((rows, d), F32).at[0].set(c_ctx).at[1:1 + bl].set(c)
    mod = _modulation(cc, w_mod, b_mod).reshape(depth, rows, 6, 1, d)

    def mods_of(l, lo, hi):
        names = ("sh1", "sc1", "g1", "sh2", "sc2", "g2")
        return {n: mod[l, lo:hi, j] for j, n in enumerate(names)}

    c_segs = [(3 * wc, 3 * wc), (rw, rwp), (ra, rap), (rg, rg)]
    nab = 2 * wa + 2 * wb
    big = {"w_in_ab": w_in[:, :, :nab].astype(BF16), "w_in_c": _pad_cols(w_in[:, :, nab:], c_segs).astype(BF16),
           "w_out": w_out.astype(BF16), "w_up": w_up.astype(BF16), "w_down": w_down.astype(BF16)}
    layers = []
    for l in range(depth):
        layers.append({
            "layer": l, **big,
            "conv_a_w": conv_a_w[l], "conv_a_b": conv_a_b[l].reshape(1, wa),
            "lru_ga_w": _block_diag_pairs(lru_ga_w[l]).astype(BF16), "lru_ga_b": lru_ga_b[l].reshape(2, 1, wa),
            "lru_gx_w": _block_diag_pairs(lru_gx_w[l]).astype(BF16), "lru_gx_b": lru_gx_b[l].reshape(2, 1, wa),
            "lru_lambda": lru_lambda[l].reshape(2, 1, wa),
            "conv_b_w": conv_b_w[l], "conv_b_b": conv_b_b[l].reshape(1, wb),
            "conv_ln_g": conv_ln_g[l].reshape(1, wb), "conv_ln_b": conv_ln_b[l].reshape(1, wb),
            "ts_mu_prev": _pad_cols(ts_mu_prev[l].reshape(1, -1), c_segs),
            "ts_mu_next": _pad_cols(ts_mu_next[l].reshape(1, -1), c_segs),
            "rwkv_w0": rwkv_w0[l].reshape(2, 1, wc),
            "rwkv_w_up": jnp.pad(rwkv_w_up[l], ((0, 0), (0, rwp - rw), (0, 0))).astype(BF16),
            "rwkv_a0": rwkv_a0[l].reshape(2, 1, wc),
            "rwkv_a_up": jnp.pad(rwkv_a_up[l], ((0, 0), (0, rap - ra), (0, 0))).astype(BF16),
            "rwkv_g_up": rwkv_g_up[l].astype(BF16),
            "rwkv_k_k": rwkv_k_k[l].reshape(1, wc), "rwkv_k_a": rwkv_k_a[l].reshape(1, wc),
            "rwkv_r_k": rwkv_r_k[l].reshape(1, wc),
            "rwkv_lnx_g": rwkv_lnx_g[l].reshape(1, wc), "rwkv_lnx_b": rwkv_lnx_b[l].reshape(1, wc),
            "ln1_g": ln1_g[l], "ln1_b": ln1_b[l], "ln2_g": ln2_g[l], "ln2_b": ln2_b[l],
        })

    m0c, m0l = mods_of(0, 0, 1), mods_of(0, 1, 1 + bl)
    xp, hp = _ln_mod(x_prompt.reshape(bp * tp, d), tp, ln_in_g, ln_in_b, mod=(m0c["sc1"], m0c["sh1"]))
    xs, hs = _ln_mod(x_sample.reshape(bl * tl, d), tl, ln_in_g, ln_in_b, mod=(m0l["sc1"], m0l["sh1"]))
    h0_ctx = jnp.zeros((bp, 2, wa), F32)
    s0_ctx = jnp.zeros((bp, 2, nheads, HEAD, HEAD), F32)
    lru_states, wkv_states = [], []
    for l in range(depth):
        mc, ml = mods_of(l, 0, 1), mods_of(l, 1, 1 + bl)
        if l + 1 < depth:
            nc_, nl_ = mods_of(l + 1, 0, 1), mods_of(l + 1, 1, 1 + bl)
            nxt_c, nxt_l = (nc_["sc1"], nc_["sh1"]), (nl_["sc1"], nl_["sh1"])
        else:
            nxt_c = nxt_l = None
        xp, hp, st_lru, st_wkv = _layer(xp, hp, bp, tp, mc, nxt_c, layers[l], h0_ctx, s0_ctx, False, True, alpha)
        lru_states.append(st_lru)
        wkv_states.append(st_wkv)
        xs, hs, _, _ = _layer(xs, hs, bl, tl, ml, nxt_l, layers[l], state_lru[:, l].astype(F32),
                              state_wkv[:, l].astype(F32), True, False, alpha)
    return (xp.reshape(bp, tp, d), xs.reshape(bl, tl, d),
            jnp.stack(lru_states, axis=1), jnp.stack(wkv_states, axis=1))
```

```python
import functools
import math

import jax
import jax.numpy as jnp
from jax import lax
from jax.experimental import pallas as pl
from jax.experimental.pallas import tpu as pltpu

F32 = jnp.float32
BF16 = jnp.bfloat16
HI = lax.Precision.HIGHEST

LANES = 128
SUBLANES = 8
VMEM_LIMIT = 60 * 1024 * 1024
GRID_W = 64
HEAD = 64
CHUNK = 64
C_LRU = 8.0
LN_EPS = 1e-5
GN_EPS = 64e-5


def _pick(n, cands):
    for c in cands:
        if c <= n and n % c == 0:
            return c
    return n


def _round_up(n, m):
    return (n + m - 1) // m * m


def _params(*sem):
    return pltpu.CompilerParams(dimension_semantics=sem, vmem_limit_bytes=VMEM_LIMIT)


def _sigmoid(x):
    return 0.5 * jnp.tanh(0.5 * x) + 0.5


def _softplus(x):
    return jnp.maximum(x, 0.0) + jnp.log(1.0 + jnp.exp(-jnp.abs(x)))


def _gelu_tanh(x):
    return 0.5 * x * (1.0 + jnp.tanh(math.sqrt(2.0 / math.pi) * (x + 0.044715 * (x * x * x))))


def _head_ones():
    r = lax.broadcasted_iota(jnp.int32, (LANES, LANES), 0) // HEAD
    c = lax.broadcasted_iota(jnp.int32, (LANES, LANES), 1) // HEAD
    return (r == c).astype(F32)


def _head_sum(x, ones):
    parts = [jnp.dot(x[:, j * LANES:(j + 1) * LANES], ones, precision=HI, preferred_element_type=F32)
             for j in range(x.shape[1] // LANES)]
    return parts[0] if len(parts) == 1 else jnp.concatenate(parts, axis=1)


def _mod_body(c_ref, w_ref, b_ref, o_ref):
    c = c_ref[...]
    a = (c * _sigmoid(c)).astype(BF16)
    o_ref[0] = jnp.dot(a, w_ref[0].astype(BF16), preferred_element_type=F32) + b_ref[0]


def _modulation(cc, w_mod, b_mod):
    depth, d, n = w_mod.shape
    rows = cc.shape[0]
    tn = _pick(n, (512, 256, 128))
    return pl.pallas_call(
        _mod_body,
        grid=(depth, n // tn),
        in_specs=[pl.BlockSpec((rows, d), lambda l, j: (0, 0)),
                  pl.BlockSpec((1, d, tn), lambda l, j: (l, 0, j)),
                  pl.BlockSpec((1, 1, tn), lambda l, j: (l, 0, j))],
        out_specs=pl.BlockSpec((1, rows, tn), lambda l, j: (l, 0, j)),
        out_shape=jax.ShapeDtypeStruct((depth, rows, n), F32),
        compiler_params=_params("parallel", "parallel"),
        name="modulation",
    )(cc, w_mod, b_mod.reshape(depth, 1, n))


def _mm_body(a_ref, w_ref, *rest, nk, relu2, alpha):
    if alpha is not None:
        x_ref, gate_ref, o_ref, *scratch = rest
    else:
        o_ref, *scratch = rest

    def finish(r):
        if relu2:
            r = jnp.square(jnp.maximum(r, 0.0))
        if alpha is not None:
            r = alpha * x_ref[...] + gate_ref[0] * r
        o_ref[...] = r.astype(o_ref.dtype)

    if nk == 1:
        finish(jnp.dot(a_ref[...], w_ref[...], preferred_element_type=F32))
        return
    acc_ref, = scratch
    k = pl.program_id(2)

    @pl.when(k == 0)
    def _():
        acc_ref[...] = jnp.zeros_like(acc_ref)

    acc_ref[...] += jnp.dot(a_ref[...], w_ref[...], preferred_element_type=F32)

    @pl.when(k == nk - 1)
    def _():
        finish(acc_ref[...])


def _matmul(a, w, layer, out_dtype, relu2=False, res=None, name="matmul"):
    m, kdim = a.shape
    n = w.shape[2]
    per_seq = res is not None and res[1].shape[0] > 1
    seq_len = res[2] if per_seq else m
    tm = _pick(seq_len, (1024, 512, 256, 128, 64, 32, 16))
    tn = _pick(n, (1024, 512, 256, 128))
    tk = _pick(kdim, (4096, 2048, 1024, 512, 256, 128))
    nk = kdim // tk
    args = [a, w]
    specs = [pl.BlockSpec((tm, tk), lambda i, j, k: (i, k)),
             pl.BlockSpec((None, tk, tn), lambda i, j, k: (layer, k, j))]
    alpha = None
    if res is not None:
        x, gate, _, alpha = res
        args += [x, gate]
        specs += [pl.BlockSpec((tm, tn), lambda i, j, k: (i, j)),
                  pl.BlockSpec((1, 1, tn), lambda i, j, k: ((i * tm) // seq_len if per_seq else 0, 0, j))]
    return pl.pallas_call(
        functools.partial(_mm_body, nk=nk, relu2=relu2, alpha=alpha),
        grid=(m // tm, n // tn, nk),
        in_specs=specs,
        out_specs=pl.BlockSpec((tm, tn), lambda i, j, k: (i, j)),
        out_shape=jax.ShapeDtypeStruct((m, n), out_dtype),
        scratch_shapes=[pltpu.VMEM((tm, tn), F32)] if nk > 1 else [],
        compiler_params=_params("parallel", "parallel", "arbitrary"),
        name=name,
    )(*args)


def _ln_body(*refs, has_mod):
    it = iter(refs)
    x_ref, g_ref, b_ref = next(it), next(it), next(it)
    if has_mod:
        sc_ref, sh_ref = next(it), next(it)
    xo_ref = next(it)
    x = x_ref[...]
    mu = jnp.mean(x, axis=-1, keepdims=True)
    xc = x - mu
    var = jnp.mean(xc * xc, axis=-1, keepdims=True)
    xn = xc * lax.rsqrt(var + LN_EPS) * g_ref[...] + b_ref[...]
    xo_ref[...] = xn
    if has_mod:
        h_ref = next(it)
        h_ref[...] = (xn * (1.0 + sc_ref[0]) + sh_ref[0]).astype(BF16)


def _ln_mod(x, seq_len, g, b, mod=None):
    m, d = x.shape
    tm = _pick(seq_len, (256, 128, 64, 32, 16))

    def per_seq(arr):
        if arr.shape[0] == 1:
            return pl.BlockSpec((1, 1, d), lambda i: (0, 0, 0))
        return pl.BlockSpec((1, 1, d), lambda i: ((i * tm) // seq_len, 0, 0))

    row = pl.BlockSpec((tm, d), lambda i: (i, 0))
    vec = pl.BlockSpec((1, d), lambda i: (0, 0))
    args, specs = [x, g.reshape(1, d), b.reshape(1, d)], [row, vec, vec]
    if mod is not None:
        args += [mod[0], mod[1]]
        specs += [per_seq(mod[0]), per_seq(mod[1])]
    out_shape = [jax.ShapeDtypeStruct((m, d), F32)]
    out_specs = [row]
    if mod is not None:
        out_shape.append(jax.ShapeDtypeStruct((m, d), BF16))
        out_specs.append(row)
    out = pl.pallas_call(
        functools.partial(_ln_body, has_mod=mod is not None),
        grid=(m // tm,),
        in_specs=specs,
        out_specs=out_specs,
        out_shape=out_shape,
        compiler_params=_params("parallel"),
        name="ln_mod",
    )(*args)
    return out if mod is not None else (out[0], None)


def _halo_specs(tm, width, col, m):
    nb8 = m // SUBLANES
    r8 = tm // SUBLANES
    return [pl.BlockSpec((tm, width), lambda i: (i, col)),
            pl.BlockSpec((SUBLANES, width), lambda i: (jnp.maximum(i * r8 - 1, 0), col)),
            pl.BlockSpec((SUBLANES, width), lambda i: (jnp.minimum((i + 1) * r8, nb8 - 1), col))]


def _fill_ext(ext_ref, z_ref, zp_ref, zn_ref, tm, seq_len):
    i = pl.program_id(0)
    first = (i * tm) % seq_len == 0
    last = ((i + 1) * tm) % seq_len == 0
    ext_ref[0:SUBLANES] = jnp.where(first, 0.0, zp_ref[...])
    ext_ref[SUBLANES:SUBLANES + tm] = z_ref[...]
    ext_ref[SUBLANES + tm:2 * SUBLANES + tm] = jnp.where(last, 0.0, zn_ref[...])


def _lru_gates_body(zx_ref, zp_ref, zn_ref, cw_ref, cb_ref, gaw_ref, gab_ref, gxw_ref, gxb_ref, lam_ref,
                    o_ref, ext_ref, *, tm, seq_len, nj, taps):
    _fill_ext(ext_ref, zx_ref, zp_ref, zn_ref, tm, seq_len)
    u = cb_ref[...] + cw_ref[0:1, :] * ext_ref[SUBLANES - 1:SUBLANES - 1 + tm, :]
    for k in range(1, taps):
        u = u + cw_ref[k:k + 1, :] * ext_ref[SUBLANES - 1 + k:SUBLANES - 1 + k + tm, :]
    for j in range(nj):
        sl = slice(j * LANES, (j + 1) * LANES)
        uj = u[:, sl]
        ub = uj.astype(BF16)
        for d in range(2):
            r = _sigmoid(jnp.dot(ub, gaw_ref[d, j], preferred_element_type=F32) + gab_ref[d][:, sl])
            ig = _sigmoid(jnp.dot(ub, gxw_ref[d, j], preferred_element_type=F32) + gxb_ref[d][:, sl])
            log_a = (-C_LRU) * r * _softplus(-lam_ref[d][:, sl])
            a = jnp.exp(log_a)
            bb = jnp.sqrt(-jnp.tanh(log_a) * (a * a + 1.0)) * (ig * uj)
            o_ref[2 * d, :, sl] = a
            o_ref[2 * d + 1, :, sl] = bb


def _lru_gates(z_ab, seq_len, p):
    m = z_ab.shape[0]
    wa = p["conv_a_w"].shape[1]
    taps = p["conv_a_w"].shape[0]
    nj = wa // LANES
    tm = _pick(seq_len, (256, 128, 64, 32, 16))
    full = lambda a: pl.BlockSpec(a.shape, lambda i: (0,) * a.ndim)
    consts = [p["conv_a_w"], p["conv_a_b"], p["lru_ga_w"], p["lru_ga_b"], p["lru_gx_w"], p["lru_gx_b"],
              p["lru_lambda"]]
    return pl.pallas_call(
        functools.partial(_lru_gates_body, tm=tm, seq_len=seq_len, nj=nj, taps=taps),
        grid=(m // tm,),
        in_specs=_halo_specs(tm, wa, 0, m) + [full(a) for a in consts],
        out_specs=pl.BlockSpec((4, tm, wa), lambda i: (0, i, 0)),
        out_shape=jax.ShapeDtypeStruct((4, m, wa), F32),
        scratch_shapes=[pltpu.VMEM((tm + 2 * SUBLANES, wa), F32)],
        compiler_params=_params("parallel"),
        name="lru_gates",
    )(z_ab, z_ab, z_ab, *consts)


def _lru_scan_body(af_ref, bf_ref, ab_ref, bb_ref, h0_ref, hf_ref, hb_ref, st_ref, carry_ref, *, tt, nt):
    i = pl.program_id(1)

    @pl.when(i == 0)
    def _():
        carry_ref[...] = h0_ref[0]

    def step(t, c):
        hf, hb = c
        hf = af_ref[0, 0, t] * hf + bf_ref[0, 0, t]
        hf_ref[0, t] = hf
        tb = tt - 1 - t
        hb = ab_ref[0, 0, tb] * hb + bb_ref[0, 0, tb]
        hb_ref[0, tb] = hb
        return hf, hb

    hf, hb = lax.fori_loop(0, tt, step, (carry_ref[0], carry_ref[1]), unroll=8)
    carry_ref[0] = hf
    carry_ref[1] = hb

    @pl.when(i == nt - 1)
    def _():
        st_ref[0, 0] = hf
        st_ref[0, 1] = hb


def _lru_scan(ab, h0, bsz, seq_len):
    wa = ab.shape[-1]
    s8 = wa // LANES
    ab5 = ab.reshape(4, bsz, seq_len, s8, LANES)
    h0 = h0.reshape(bsz, 2, s8, LANES)
    tt = _pick(seq_len, (256, 128, 64, 32, 16, 8))
    nt = seq_len // tt
    blk = (1, 1, tt, s8, LANES)
    oblk = (1, tt, s8, LANES)
    hf, hb, st = pl.pallas_call(
        functools.partial(_lru_scan_body, tt=tt, nt=nt),
        grid=(bsz, nt),
        in_specs=[pl.BlockSpec(blk, lambda b, i: (0, b, i, 0, 0)),
                  pl.BlockSpec(blk, lambda b, i: (1, b, i, 0, 0)),
                  pl.BlockSpec(blk, lambda b, i: (2, b, nt - 1 - i, 0, 0)),
                  pl.BlockSpec(blk, lambda b, i: (3, b, nt - 1 - i, 0, 0)),
                  pl.BlockSpec((1, 2, s8, LANES), lambda b, i: (b, 0, 0, 0))],
        out_specs=[pl.BlockSpec(oblk, lambda b, i: (b, i, 0, 0)),
                   pl.BlockSpec(oblk, lambda b, i: (b, nt - 1 - i, 0, 0)),
                   pl.BlockSpec((1, 2, s8, LANES), lambda b, i: (b, 0, 0, 0))],
        out_shape=[jax.ShapeDtypeStruct((bsz, seq_len, s8, LANES), F32),
                   jax.ShapeDtypeStruct((bsz, seq_len, s8, LANES), F32),
                   jax.ShapeDtypeStruct((bsz, 2, s8, LANES), F32)],
        scratch_shapes=[pltpu.VMEM((2, s8, LANES), F32)],
        compiler_params=_params("parallel", "arbitrary"),
        name="lru_scan",
    )(ab5, ab5, ab5, ab5, h0)
    m = bsz * seq_len
    return hf.reshape(m, wa), hb.reshape(m, wa), st.reshape(bsz, 2, wa)


def _convb_body(zv_ref, zg_ref, w_ref, b_ref, o_ref, pad_ref, *, nseq, seq_len, pad, stride, taps, rc):
    zeros = jnp.zeros((pad, LANES), F32)
    half = taps // 2
    nchunks = seq_len // rc
    for s in range(nseq):
        pad_ref[s, 0:pad] = zeros
        pad_ref[s, pad + seq_len:2 * pad + seq_len] = zeros
        pad_ref[s, pad:pad + seq_len] = zv_ref[s] * _sigmoid(zg_ref[s])

        def rows(r0, s=s):
            acc = jnp.broadcast_to(b_ref[...], (rc, LANES))
            for k in range(taps):
                acc = acc + w_ref[k:k + 1, :] * pad_ref[s, pl.ds(r0 + pad + (k - half) * stride, rc), :]
            o_ref[s, pl.ds(r0, rc), :] = acc

        if stride == 1:
            for c in range(nchunks):
                rows(c * rc)
        else:
            def chunk(c, carry, rows=rows):
                rows(pl.multiple_of(c * rc, rc))
                return carry
            lax.fori_loop(0, nchunks, chunk, 0)


def _convb(z_ab, bsz, seq_len, col0, p, on_grid):
    wb = p["conv_b_w"].shape[1]
    taps = p["conv_b_w"].shape[0]
    nj = wb // LANES
    stride = GRID_W if on_grid else 1
    pad = _round_up((taps // 2) * stride, SUBLANES * 2)
    rc = _pick(seq_len, (64, 32, 16, 8)) if stride == 1 else _pick(seq_len, (256, 128, 64))
    z3 = z_ab.reshape(bsz, seq_len, z_ab.shape[1])
    cv, cg = col0 // LANES, (col0 + wb) // LANES
    nseq = 1 if on_grid else _pick(bsz, (4, 2))
    out = pl.pallas_call(
        functools.partial(_convb_body, nseq=nseq, seq_len=seq_len, pad=pad, stride=stride, taps=taps, rc=rc),
        grid=(bsz // nseq, nj),
        in_specs=[pl.BlockSpec((nseq, seq_len, LANES), lambda b, j: (b, 0, cv + j)),
                  pl.BlockSpec((nseq, seq_len, LANES), lambda b, j: (b, 0, cg + j)),
                  pl.BlockSpec((taps, LANES), lambda b, j: (0, j)),
                  pl.BlockSpec((1, LANES), lambda b, j: (0, j))],
        out_specs=pl.BlockSpec((nseq, seq_len, LANES), lambda b, j: (b, 0, j)),
        out_shape=jax.ShapeDtypeStruct((bsz, seq_len, wb), F32),
        scratch_shapes=[pltpu.VMEM((nseq, seq_len + 2 * pad, LANES), F32)],
        compiler_params=_params("parallel", "parallel"),
        name="convb_grid" if on_grid else "convb_seq",
    )(z3, z3, p["conv_b_w"], p["conv_b_b"])
    return out.reshape(bsz * seq_len, wb)


def _prep_body(z_ref, zp_ref, zn_ref, mup_ref, mun_ref, w0_ref, wup_ref, a0_ref, aup_ref, gup_ref,
               kk_ref, ka_ref, rk_ref, sh_ref, lw_ref, pd_ref, ext_ref, *, tm, seq_len, wc, rw, ra):
    _fill_ext(ext_ref, z_ref, zp_ref, zn_ref, tm, seq_len)
    ones = _head_ones()

    def shifted(c0, c1):
        z = ext_ref[SUBLANES:SUBLANES + tm, c0:c1]
        zp = ext_ref[SUBLANES - 1:SUBLANES - 1 + tm, c0:c1]
        zn = ext_ref[SUBLANES + 1:SUBLANES + 1 + tm, c0:c1]
        return z + mup_ref[:, c0:c1] * (zp - z) + mun_ref[:, c0:c1] * (zn - z)

    r = shifted(0, wc)
    k = shifted(wc, 2 * wc)
    v = shifted(2 * wc, 3 * wc)
    xw = shifted(3 * wc, 3 * wc + rw)
    xa = shifted(3 * wc + rw, 3 * wc + rw + ra)
    xg = shifted(3 * wc + rw + ra, z_ref.shape[1])

    sh_ref[0] = r.astype(BF16)
    sh_ref[1] = v.astype(BF16)
    kk = k * kk_ref[...]
    kap = kk / jnp.maximum(jnp.sqrt(_head_sum(kk * kk, ones)), 1e-12)
    sh_ref[2] = kap.astype(BF16)
    sh_ref[3] = jnp.dot(_sigmoid(xg).astype(BF16), gup_ref[...], preferred_element_type=F32).astype(BF16)

    tw = jnp.tanh(xw).astype(BF16)
    xab = xa.astype(BF16)
    kdsum = None
    for d in range(2):
        x = w0_ref[d] + jnp.dot(tw, wup_ref[d], preferred_element_type=F32)
        lw_ref[d] = (-math.exp(-0.5)) * _sigmoid(x)
        eta = _sigmoid(a0_ref[d] + jnp.dot(xab, aup_ref[d], preferred_element_type=F32))
        pd_ref[d, 0] = (kap * eta).astype(BF16)
        kd = k * (1.0 + (eta - 1.0) * ka_ref[...])
        pd_ref[d, 1] = kd.astype(BF16)
        kdsum = kd if kdsum is None else kdsum + kd
    sh_ref[4] = (_head_sum(r * kdsum * rk_ref[...], ones) * v).astype(BF16)


def _rwkv_prep(z_c, seq_len, p):
    m, wz = z_c.shape
    wc = p["rwkv_k_k"].shape[1]
    rw, ra = p["rwkv_w_up"].shape[1], p["rwkv_a_up"].shape[1]
    tm = _pick(seq_len, (128, 64, 32, 16))
    full = lambda a: pl.BlockSpec(a.shape, lambda i: (0,) * a.ndim)
    consts = [p["ts_mu_prev"], p["ts_mu_next"], p["rwkv_w0"], p["rwkv_w_up"], p["rwkv_a0"], p["rwkv_a_up"],
              p["rwkv_g_up"], p["rwkv_k_k"], p["rwkv_k_a"], p["rwkv_r_k"]]
    return pl.pallas_call(
        functools.partial(_prep_body, tm=tm, seq_len=seq_len, wc=wc, rw=rw, ra=ra),
        grid=(m // tm,),
        in_specs=_halo_specs(tm, wz, 0, m) + [full(a) for a in consts],
        out_specs=[pl.BlockSpec((5, tm, wc), lambda i: (0, i, 0)),
                   pl.BlockSpec((2, tm, wc), lambda i: (0, i, 0)),
                   pl.BlockSpec((2, 2, tm, wc), lambda i: (0, 0, i, 0))],
        out_shape=[jax.ShapeDtypeStruct((5, m, wc), BF16),
                   jax.ShapeDtypeStruct((2, m, wc), F32),
                   jax.ShapeDtypeStruct((2, 2, m, wc), BF16)],
        scratch_shapes=[pltpu.VMEM((tm + 2 * SUBLANES, wz), F32)],
        compiler_params=_params("parallel"),
        name="rwkv_prep",
    )(z_c, z_c, z_c, *consts)


_NN = (((1,), (0,)), ((), ()))
_NT = (((1,), (1,)), ((), ()))
_TN = (((0,), (0,)), ((), ()))
WKV_GROUP = 16


def _split(x, pieces):
    hi = x.astype(BF16)
    if pieces == 1:
        return (hi,)
    return (hi, (x - hi.astype(F32)).astype(BF16))


def _mm(xs, ys, dims=_NN):
    ax_x, ax_y = dims[0][0][0], dims[0][1][0]
    if len(xs) == 1:
        x, y = xs[0], ys[0]
    else:
        x = jnp.concatenate([xs[0], xs[1], xs[0]], axis=ax_x)
        y = jnp.concatenate([ys[0], ys[0], ys[1]], axis=ax_y)
    return lax.dot_general(x, y, dims, preferred_element_type=F32)


def _wkv_body(r_ref, v_ref, kap_ref, lw_ref, ke_ref, kd_ref, s0_ref, o_ref, so_ref,
              s_ref, a_sc, b_sc, k_sc, r_sc, b2_sc, k2_sc, g_sc, *, nc, npairs, group, inv_pieces):
    c = CHUNK
    d = pl.program_id(1)
    ci = pl.program_id(2)

    @pl.when(ci == 0)
    def _():
        zero = jnp.zeros((HEAD, HEAD), F32)
        for p in range(npairs):
            top = jnp.concatenate([s0_ref[0, 0, 2 * p], zero], axis=1)
            bot = jnp.concatenate([zero, s0_ref[0, 0, 2 * p + 1]], axis=1)
            s_ref[p] = jnp.concatenate([top, bot], axis=0)

    row = lax.broadcasted_iota(jnp.int32, (c, 2 * c), 0)
    col = lax.broadcasted_iota(jnp.int32, (c, 2 * c), 1) % c
    fwd = d == 0
    ahead = jnp.where(fwd, row - col, col - row)
    strict2 = ahead > 0
    incl2 = ahead >= 0
    lw = lw_ref[0]
    l1 = lw.astype(BF16)
    rem = lw - l1.astype(F32)
    l2 = rem.astype(BF16)
    l3 = (rem - l2.astype(F32)).astype(BF16)
    tri = incl2[:, :c].astype(F32).astype(BF16)
    cum = jnp.dot(jnp.concatenate([tri, tri, tri], axis=1), jnp.concatenate([l1, l2, l3], axis=0),
                  preferred_element_type=F32)
    tot = jnp.where(fwd, cum[c - 1:c, :], cum[0:1, :])
    ke = ke_ref[0, 0].astype(F32)
    kd = kd_ref[0, 0].astype(F32)
    gtot = jnp.exp(tot)
    a_sc[...] = -kap_ref[0].astype(F32) * jnp.exp(cum - lw)
    em = jnp.exp(-cum)
    b_sc[...] = ke * em
    k_sc[...] = kd * em
    r_sc[...] = r_ref[0].astype(F32) * jnp.exp(cum)
    e2 = gtot * em
    b2_sc[...] = ke * e2
    k2_sc[...] = kd * e2
    g_sc[...] = jnp.broadcast_to(gtot, g_sc.shape)

    lane = lax.broadcasted_iota(jnp.int32, (1, LANES), 1)
    m0 = lane < HEAD
    rbd = lax.broadcasted_iota(jnp.int32, (LANES, LANES), 0) // HEAD
    cbd = lax.broadcasted_iota(jnp.int32, (LANES, LANES), 1) // HEAD
    bdmask = rbd == cbd
    m0w = jnp.concatenate([m0, m0], axis=1)
    levels = c.bit_length() - 1

    def stack(x):
        m = m0 if x.shape[1] == LANES else m0w
        return jnp.concatenate([jnp.where(m, x, 0.0), jnp.where(m, 0.0, x)], axis=0)

    def bdot(x, y, dims=_NN):
        return lax.dot_general(x.astype(BF16), y.astype(BF16), dims, preferred_element_type=F32)

    def process(sls, sidx):
        n = len(sls)
        mbd, xbd, lr, vst, sbd = [], [], [], [], []
        for q in range(n):
            sl = sls[q]
            lm = bdot(jnp.concatenate([a_sc[:, sl], r_sc[:, sl]], axis=0),
                      jnp.concatenate([stack(b_sc[:, sl]), stack(k_sc[:, sl])], axis=0), _NT)
            lak = jnp.where(strict2, lm[:c, 2 * c:], 0.0)
            lr.append(jnp.concatenate([jnp.where(incl2, lm[c:, :2 * c], 0.0),
                                       jnp.where(incl2, lm[c:, 2 * c:], 0.0)], axis=1).astype(BF16))
            vst.append(stack(v_ref[0, :, sl].astype(F32)).astype(BF16))
            sbd.append(s_ref[sidx[q]].astype(BF16))
            xbd.append(lak)
            mbd.append(jnp.where(strict2, lm[:c, :2 * c], 0.0))
        for q in range(n):
            xbd[q] = bdot(a_sc[:, sls[q]], sbd[q], _NT) + bdot(xbd[q], vst[q])
        for lvl in range(levels):
            for q in range(n):
                lhs = _split(mbd[q], inv_pieces)
                if lvl + 1 < levels:
                    y = _mm(lhs, _split(stack(jnp.concatenate([xbd[q], mbd[q]], axis=1)), inv_pieces))
                    xbd[q] = xbd[q] + y[:, :LANES]
                    mbd[q] = y[:, LANES:]
                else:
                    xbd[q] = xbd[q] + _mm(lhs, _split(stack(xbd[q]), inv_pieces))
        for q in range(n):
            sl = sls[q]
            u = xbd[q]
            o_ref[0, :, sl] = (bdot(r_sc[:, sl], sbd[q], _NT)
                               + bdot(lr[q], jnp.concatenate([stack(u).astype(BF16), vst[q]], axis=0))).astype(BF16)
        for q in range(n):
            sl = sls[q]
            upd = bdot(jnp.concatenate([xbd[q], v_ref[0, :, sl].astype(F32)], axis=0),
                       jnp.concatenate([b2_sc[:, sl], k2_sc[:, sl]], axis=0), _TN)
            s_ref[sidx[q]] = s_ref[sidx[q]] * g_sc[0:1, sl] + jnp.where(bdmask, upd, 0.0)

    if group >= npairs:
        process([slice(p * LANES, (p + 1) * LANES) for p in range(npairs)], list(range(npairs)))
    else:
        def body(g, carry):
            idx = [g * group + q for q in range(group)]
            process([pl.ds(pl.multiple_of(i * LANES, LANES), LANES) for i in idx], idx)
            return carry
        lax.fori_loop(0, npairs // group, body, 0)

    @pl.when(ci == nc - 1)
    def _():
        for p in range(npairs):
            sbd = s_ref[p]
            so_ref[0, 0, 2 * p] = sbd[:HEAD, :HEAD]
            so_ref[0, 0, 2 * p + 1] = sbd[HEAD:, HEAD:]


def _wkv(shared, logw, perdir, s0, bsz, seq_len, state_is_output):
    m, wc = shared.shape[1], shared.shape[2]
    nheads = wc // HEAD
    npairs = nheads // 2
    c = CHUNK
    nc = seq_len // c

    def chunk(b, d, i):
        return b * nc + jnp.where(d == 0, i, nc - 1 - i)

    sh = lambda j: pl.BlockSpec((1, c, wc), lambda b, d, i: (j, chunk(b, d, i), 0))
    pd = lambda j: pl.BlockSpec((1, 1, c, wc), lambda b, d, i: (d, j, chunk(b, d, i), 0))
    per_dir = pl.BlockSpec((1, c, wc), lambda b, d, i: (d, chunk(b, d, i), 0))
    st = pl.BlockSpec((1, 1, nheads, HEAD, HEAD), lambda b, d, i: (b, d, 0, 0, 0))
    cw = pltpu.VMEM((c, wc), F32)
    return pl.pallas_call(
        functools.partial(_wkv_body, nc=nc, npairs=npairs,
                          group=WKV_GROUP if npairs % WKV_GROUP == 0 else npairs,
                          inv_pieces=2 if state_is_output else 1),
        grid=(bsz, 2, nc),
        in_specs=[sh(0), sh(1), sh(2), per_dir, pd(0), pd(1), st],
        out_specs=[per_dir, st],
        out_shape=[jax.ShapeDtypeStruct((2, m, wc), BF16),
                   jax.ShapeDtypeStruct((bsz, 2, nheads, HEAD, HEAD), F32)],
        scratch_shapes=[pltpu.VMEM((npairs, LANES, LANES), F32), cw, cw, cw, cw, cw, cw,
                        pltpu.VMEM((SUBLANES, wc), F32)],
        compiler_params=_params("parallel", "parallel", "arbitrary"),
        name="wkv_scan",
    )(shared, shared, shared, logw, perdir, perdir, s0)


def _post_body(o_ref, g_ref, bonus_ref, hf_ref, hb_ref, zg_ref, cb_ref, lng_ref, lnb_ref, xg_ref, xb_ref,
               y_ref, *, wa, wb):
    ones = _head_ones()
    y_ref[:, 0:wa] = ((hf_ref[...] + hb_ref[...]) * _gelu_tanh(zg_ref[...])).astype(BF16)

    u = cb_ref[...]
    mu = jnp.mean(u, axis=-1, keepdims=True)
    uc = u - mu
    var = jnp.mean(uc * uc, axis=-1, keepdims=True)
    un = uc * lax.rsqrt(var + LN_EPS) * lng_ref[...] + lnb_ref[...]
    y_ref[:, wa:wa + wb] = (un * _sigmoid(un)).astype(BF16)

    o = o_ref[0].astype(F32) + o_ref[1].astype(F32)
    inv = 1.0 / HEAD
    omu = _head_sum(o, ones) * inv
    oc = o - omu
    ovar = _head_sum(oc * oc, ones) * inv
    on = oc * lax.rsqrt(ovar + GN_EPS) * xg_ref[...] + xb_ref[...]
    y_ref[:, wa + wb:] = ((on + bonus_ref[0]) * g_ref[0]).astype(BF16)


def _post(o, shared, hf, hb, z_ab, cb, seq_len, p):
    m, wc = o.shape[1], o.shape[2]
    wa, wb = hf.shape[1], cb.shape[1]
    dmix = wa + wb + wc
    tm = _pick(seq_len, (256, 128, 64, 32, 16))
    vec = lambda n: pl.BlockSpec((1, n), lambda i: (0, 0))
    return pl.pallas_call(
        functools.partial(_post_body, wa=wa, wb=wb),
        grid=(m // tm,),
        in_specs=[pl.BlockSpec((2, tm, wc), lambda i: (0, i, 0)),
                  pl.BlockSpec((1, tm, wc), lambda i: (3, i, 0)),
                  pl.BlockSpec((1, tm, wc), lambda i: (4, i, 0)),
                  pl.BlockSpec((tm, wa), lambda i: (i, 0)),
                  pl.BlockSpec((tm, wa), lambda i: (i, 0)),
                  pl.BlockSpec((tm, wa), lambda i: (i, 1)),
                  pl.BlockSpec((tm, wb), lambda i: (i, 0)),
                  vec(wb), vec(wb), vec(wc), vec(wc)],
        out_specs=pl.BlockSpec((tm, dmix), lambda i: (i, 0)),
        out_shape=jax.ShapeDtypeStruct((m, dmix), BF16),
        compiler_params=_params("parallel"),
        name="mix_post",
    )(o, shared, shared, hf, hb, z_ab, cb, p["conv_ln_g"], p["conv_ln_b"], p["rwkv_lnx_g"], p["rwkv_lnx_b"])


def _layer(x, h, bsz, seq_len, mods, nxt, p, h0_lru, s0_wkv, on_grid, keep_state, alpha):
    wa = p["conv_a_w"].shape[1]
    l = p["layer"]
    z_ab = _matmul(h, p["w_in_ab"], l, F32, name="w_in_ab")
    z_c = _matmul(h, p["w_in_c"], l, F32, name="w_in_c")
    ab = _lru_gates(z_ab, seq_len, p)
    hf, hb, st_lru = _lru_scan(ab, h0_lru, bsz, seq_len)
    cb = _convb(z_ab, bsz, seq_len, 2 * wa, p, on_grid)
    shared, logw, perdir = _rwkv_prep(z_c, seq_len, p)
    o, st_wkv = _wkv(shared, logw, perdir, s0_wkv, bsz, seq_len, state_is_output=keep_state)
    ycat = _post(o, shared, hf, hb, z_ab, cb, seq_len, p)
    pre = _matmul(ycat, p["w_out"], l, F32, res=(x, mods["g1"], seq_len, alpha), name="w_out")
    x, h2 = _ln_mod(pre, seq_len, p["ln1_g"], p["ln1_b"], mod=(mods["sc2"], mods["sh2"]))
    hid = _matmul(h2, p["w_up"], l, BF16, relu2=True, name="ffn_up")
    pre = _matmul(hid, p["w_down"], l, F32, res=(x, mods["g2"], seq_len, alpha), name="ffn_down")
    x, hn = _ln_mod(pre, seq_len, p["ln2_g"], p["ln2_b"], mod=nxt)
    return x, hn, st_lru, st_wkv


def _block_diag_pairs(w):
    two, nb, hd, _ = w.shape
    w = w.reshape(two, nb // 2, 2, hd, hd)
    z = jnp.zeros_like(w[:, :, 0])
    top = jnp.concatenate([w[:, :, 0], z], axis=-1)
    bot = jnp.concatenate([z, w[:, :, 1]], axis=-1)
    return jnp.concatenate([top, bot], axis=-2)


def _pad_cols(a, segs):
    out, c = [], 0
    for w, wp in segs:
        piece = a[..., c:c + w]
        if wp > w:
            piece = jnp.pad(piece, [(0, 0)] * (a.ndim - 1) + [(0, wp - w)])
        out.append(piece)
        c += w
    return jnp.concatenate(out, axis=-1)


def kernel(x_prompt, x_sample, state_lru, state_wkv, c, c_ctx, ln_in_g, ln_in_b, w_mod, b_mod, w_in, w_out, conv_a_w, conv_a_b, lru_ga_w, lru_ga_b, lru_gx_w, lru_gx_b, lru_lambda, conv_b_w, conv_b_b, conv_ln_g, conv_ln_b, ts_mu_prev, ts_mu_next, rwkv_w0, rwkv_w_up, rwkv_a0, rwkv_a_up, rwkv_g_up, rwkv_k_k, rwkv_k_a, rwkv_r_k, rwkv_lnx_g, rwkv_lnx_b, ln1_g, ln1_b, w_up, w_down, ln2_g, ln2_b):
    bp, tp, d = x_prompt.shape
    bl, tl, _ = x_sample.shape
    depth = w_in.shape[0]
    wa = conv_a_w.shape[-1]
    wb = conv_b_w.shape[-1]
    wc = rwkv_k_k.shape[-1]
    rw, ra, rg = rwkv_w_up.shape[2], rwkv_a_up.shape[2], rwkv_g_up.shape[1]
    rwp, rap = _round_up(rw, LANES), _round_up(ra, LANES)
    nheads = wc // HEAD
    alpha = (2 * depth) ** 0.25
    assert CHUNK == HEAD and tp % CHUNK == 0 and tl % CHUNK == 0 and tl % GRID_W == 0

    rows = _round_up(1 + bl, 2 * SUBLANES)
    cc = jnp.zeros((rows, d), F32).at[0].set(c_ctx).at[1:1 + bl].set(c)
    mod = _modulation(cc, w_mod, b_mod).reshape(depth, rows, 6, 1, d)

    def mods_of(l, lo, hi):
        names = ("sh1", "sc1", "g1", "sh2", "sc2", "g2")
        return {n: mod[l, lo:hi, j] for j, n in enumerate(names)}

    c_segs = [(3 * wc, 3 * wc), (rw, rwp), (ra, rap), (rg, rg)]
    nab = 2 * wa + 2 * wb
    big = {"w_in_ab": w_in[:, :, :nab].astype(BF16), "w_in_c": _pad_cols(w_in[:, :, nab:], c_segs).astype(BF16),
           "w_out": w_out.astype(BF16), "w_up": w_up.astype(BF16), "w_down": w_down.astype(BF16)}
    layers = []
    for l in range(depth):
        layers.append({
            "layer": l, **big,
            "conv_a_w": conv_a_w[l], "conv_a_b": conv_a_b[l].reshape(1, wa),
            "lru_ga_w": _block_diag_pairs(lru_ga_w[l]).astype(BF16), "lru_ga_b": lru_ga_b[l].reshape(2, 1, wa),
            "lru_gx_w": _block_diag_pairs(lru_gx_w[l]).astype(BF16), "lru_gx_b": lru_gx_b[l].reshape(2, 1, wa),
            "lru_lambda": lru_lambda[l].reshape(2, 1, wa),
            "conv_b_w": conv_b_w[l], "conv_b_b": conv_b_b[l].reshape(1, wb),
            "conv_ln_g": conv_ln_g[l].reshape(1, wb), "conv_ln_b": conv_ln_b[l].reshape(1, wb),
            "ts_mu_prev": _pad_cols(ts_mu_prev[l].reshape(1, -1), c_segs),
            "ts_mu_next": _pad_cols(ts_mu_next[l].reshape(1, -1), c_segs),
            "rwkv_w0": rwkv_w0[l].reshape(2, 1, wc),
            "rwkv_w_up": jnp.pad(rwkv_w_up[l], ((0, 0), (0, rwp - rw), (0, 0))).astype(BF16),
            "rwkv_a0": rwkv_a0[l].reshape(2, 1, wc),
            "rwkv_a_up": jnp.pad(rwkv_a_up[l], ((0, 0), (0, rap - ra), (0, 0))).astype(BF16),
            "rwkv_g_up": rwkv_g_up[l].astype(BF16),
            "rwkv_k_k": rwkv_k_k[l].reshape(1, wc), "rwkv_k_a": rwkv_k_a[l].reshape(1, wc),
            "rwkv_r_k": rwkv_r_k[l].reshape(1, wc),
            "rwkv_lnx_g": rwkv_lnx_g[l].reshape(1, wc), "rwkv_lnx_b": rwkv_lnx_b[l].reshape(1, wc),
            "ln1_g": ln1_g[l], "ln1_b": ln1_b[l], "ln2_g": ln2_g[l], "ln2_b": ln2_b[l],
        })

    m0c, m0l = mods_of(0, 0, 1), mods_of(0, 1, 1 + bl)
    xp, hp = _ln_mod(x_prompt.reshape(bp * tp, d), tp, ln_in_g, ln_in_b, mod=(m0c["sc1"], m0c["sh1"]))
    xs, hs = _ln_mod(x_sample.reshape(bl * tl, d), tl, ln_in_g, ln_in_b, mod=(m0l["sc1"], m0l["sh1"]))
    h0_ctx = jnp.zeros((bp, 2, wa), F32)
    s0_ctx = jnp.zeros((bp, 2, nheads, HEAD, HEAD), F32)
    lru_states, wkv_states = [], []
    for l in range(depth):
        mc, ml = mods_of(l, 0, 1), mods_of(l, 1, 1 + bl)
        if l + 1 < depth:
            nc_, nl_ = mods_of(l + 1, 0, 1), mods_of(l + 1, 1, 1 + bl)
            nxt_c, nxt_l = (nc_["sc1"], nc_["sh1"]), (nl_["sc1"], nl_["sh1"])
        else:
            nxt_c = nxt_l = None
        xp, hp, st_lru, st_wkv = _layer(xp, hp, bp, tp, mc, nxt_c, layers[l], h0_ctx, s0_ctx, False, True, alpha)
        lru_states.append(st_lru)
        wkv_states.append(st_wkv)
        xs, hs, _, _ = _layer(xs, hs, bl, tl, ml, nxt_l, layers[l], state_lru[:, l].astype(F32),
                              state_wkv[:, l].astype(F32), True, False, alpha)
    return (xp.reshape(bp, tp, d), xs.reshape(bl, tl, d),
            jnp.stack(lru_states, axis=1), jnp.stack(wkv_states, axis=1))
```
